```python
import math
import jax, jax.numpy as jnp
from jax import lax
import numpy as np

D_MODEL = 2048
BATCH = 2
SEQ = 16384
DEPTH = 1

D_M = D_MODEL
N_M_HEADS = 8
M_V_DIM = D_M // N_M_HEADS
M_QK_DIM = M_V_DIM // 2
D_QK = N_M_HEADS * M_QK_DIM
D_G = D_MODEL
N_G_HEADS = 8
G_DIM = D_G // N_G_HEADS
D_MIX = D_M + D_G
CHUNK = 128
CONV_W = 4
EPS = 1e-6

COL_SIZES = (D_QK, D_QK, D_M, D_M, D_M, N_M_HEADS, N_M_HEADS, D_G, D_G, D_G)
D_IN = sum(COL_SIZES)

kernel_name = "hymba_mlstm_gmlp_hybrid_layer"


def _rmsnorm(x, w):
    xf = x.astype(jnp.float32)
    r = xf * lax.rsqrt(jnp.mean(xf * xf, axis=-1, keepdims=True) + EPS)
    return (r * w.astype(jnp.float32)).astype(x.dtype)


def _split_cols(h):
    offs = np.cumsum((0,) + COL_SIZES)
    return [h[..., int(offs[i]):int(offs[i + 1])] for i in range(len(COL_SIZES))]


def _causal_depthwise_conv(x, w, b):
    S = x.shape[1]
    xp = jnp.pad(x, ((0, 0), (CONV_W - 1, 0), (0, 0)))
    y = b
    for j in range(CONV_W):
        y = y + w[j] * lax.dynamic_slice_in_dim(xp, j, S, axis=1)
    return y


def _mlstm_chunkwise(q, k, v, log_i, log_f):
    B, S, H, DK = q.shape
    DV = v.shape[-1]
    NC = S // CHUNK

    def to_chunks(a):
        a = a.reshape((B, NC, CHUNK, H) + a.shape[3:])
        perm = (1, 0, 3, 2) + tuple(range(4, a.ndim))
        return a.transpose(perm)

    qc, kc, vc = to_chunks(q), to_chunks(k), to_chunks(v)
    lic, lfc = to_chunks(log_i), to_chunks(log_f)
    causal = jnp.tril(jnp.ones((CHUNK, CHUNK), dtype=bool))

    def step(carry, inp):
        C, n, m = carry
        qb, kb, vb, li, lf = inp
        b = jnp.cumsum(lf, axis=-1)
        D = b[..., :, None] - b[..., None, :] + li[..., None, :]
        D = jnp.where(causal, D, -jnp.inf)
        inter = b + m[..., None]
        m_t = jnp.maximum(inter, jnp.max(D, axis=-1))
        decay = jnp.exp(inter - m_t)
        P = jnp.exp(D - m_t[..., None])
        Sqk = jnp.einsum('bhtd,bhsd->bhts', qb, kb) * P
        num = jnp.einsum('bhts,bhsv->bhtv', Sqk, vb) + decay[..., None] * jnp.einsum('bhtd,bhdv->bhtv', qb, C)
        den = jnp.sum(Sqk, axis=-1) + decay * jnp.einsum('bhtd,bhd->bht', qb, n)
        h = num / jnp.maximum(jnp.abs(den), jnp.exp(-m_t))[..., None]
        bL = b[..., -1]
        wlog = bL[..., None] - b + li
        m_new = jnp.maximum(bL + m, jnp.max(wlog, axis=-1))
        a = jnp.exp(bL + m - m_new)
        ws = jnp.exp(wlog - m_new[..., None])
        C_new = a[..., None, None] * C + jnp.einsum('bhs,bhsd,bhsv->bhdv', ws, kb, vb)
        n_new = a[..., None] * n + jnp.einsum('bhs,bhsd->bhd', ws, kb)
        return (C_new, n_new, m_new), h

    init = (jnp.zeros((B, H, DK, DV), jnp.float32),
            jnp.zeros((B, H, DK), jnp.float32),
            jnp.zeros((B, H), jnp.float32))
    _, hc = lax.scan(step, init, (qc, kc, vc, lic, lfc))
    return hc.transpose(1, 0, 3, 2, 4).reshape(B, S, H, DV)


def _layernorm(x, w, b):
    xf = x.astype(jnp.float32)
    mu = jnp.mean(xf, axis=-1, keepdims=True)
    var = jnp.mean(jnp.square(xf - mu), axis=-1, keepdims=True)
    return (xf - mu) * lax.rsqrt(var + EPS) * w + b


def setup_inputs(seed: int = 0) -> dict:
    key = jax.random.key(seed)
    ks = jax.random.split(key, 14)
    f32 = jnp.float32
    x = jax.random.normal(ks[0], (BATCH, SEQ, D_MODEL), f32)
    norm_w = 1.0 + 0.02 * jax.random.normal(ks[1], (D_MODEL,), f32)
    w_in = jax.random.normal(ks[2], (D_MODEL, D_IN), f32) * D_MODEL ** -0.5
    conv_w = jax.random.normal(ks[3], (CONV_W, 2 * D_QK), f32) * CONV_W ** -0.5
    conv_b = 0.02 * jax.random.normal(ks[4], (2 * D_QK,), f32)
    b_igate = 0.1 * jax.random.normal(ks[5], (N_M_HEADS,), f32)
    b_fgate = 3.0 + 3.0 * jax.random.uniform(ks[6], (N_M_HEADS,), f32)
    mlstm_norm_w = 1.0 + 0.02 * jax.random.normal(ks[7], (D_M,), f32)
    sgu_norm_w = 1.0 + 0.02 * jax.random.normal(ks[8], (D_G,), f32)
    sgu_norm_b = 0.02 * jax.random.normal(ks[9], (D_G,), f32)
    w_spatial = jax.random.normal(ks[10], (N_G_HEADS, CHUNK, CHUNK), f32) * CHUNK ** -0.5
    b_spatial = 1.0 + 0.1 * jax.random.normal(ks[11], (N_G_HEADS, CHUNK), f32)
    w_out = jax.random.normal(ks[12], (D_MIX, D_MODEL), f32) * D_MIX ** -0.5
    final_norm_w = 1.0 + 0.02 * jax.random.normal(ks[13], (D_MODEL,), f32)
    return {"x": x, "norm_w": norm_w, "w_in": w_in, "conv_w": conv_w, "conv_b": conv_b,
            "b_igate": b_igate, "b_fgate": b_fgate, "mlstm_norm_w": mlstm_norm_w,
            "sgu_norm_w": sgu_norm_w, "sgu_norm_b": sgu_norm_b, "w_spatial": w_spatial,
            "b_spatial": b_spatial, "w_out": w_out, "final_norm_w": final_norm_w}


def reference(x, norm_w, w_in, conv_w, conv_b, b_igate, b_fgate, mlstm_norm_w,
              sgu_norm_w, sgu_norm_b, w_spatial, b_spatial, w_out, final_norm_w):
    B, S, _ = x.shape
    NC = S // CHUNK
    f32 = jnp.float32
    h = x
    for _layer in range(DEPTH):
        xn = _rmsnorm(h, norm_w)
        proj = jnp.einsum('bsd,de->bse', xn, w_in).astype(f32)
        q, k, v_m, o_m, z_m, i_pre, f_pre, u_g, v_g, z_g = _split_cols(proj)

        qk = jax.nn.silu(_causal_depthwise_conv(jnp.concatenate([q, k], axis=-1),
                                                 conv_w.astype(f32), conv_b.astype(f32)))
        q = qk[..., :D_QK].reshape(B, S, N_M_HEADS, M_QK_DIM)
        k = qk[..., D_QK:].reshape(B, S, N_M_HEADS, M_QK_DIM) * (M_QK_DIM ** -0.5)
        v = v_m.reshape(B, S, N_M_HEADS, M_V_DIM)
        log_i = i_pre + b_igate.astype(f32)
        log_f = jax.nn.log_sigmoid(f_pre + b_fgate.astype(f32))
        hm = _mlstm_chunkwise(q, k, v, log_i, log_f)
        hm = hm * lax.rsqrt(jnp.mean(hm * hm, axis=-1, keepdims=True) + EPS)
        hm = hm.reshape(B, S, D_M) * mlstm_norm_w.astype(f32)
        y_m = hm * jax.nn.sigmoid(o_m) * jax.nn.silu(z_m)

        u = jax.nn.gelu(u_g)
        vg = jax.nn.gelu(v_g).reshape(B, S, N_G_HEADS, G_DIM)
        vg = _layernorm(vg, sgu_norm_w.reshape(N_G_HEADS, G_DIM).astype(f32),
                        sgu_norm_b.reshape(N_G_HEADS, G_DIM).astype(f32))
        vg = vg.reshape(B, NC, CHUNK, N_G_HEADS, G_DIM)
        w_s = jnp.where(jnp.tril(jnp.ones((CHUNK, CHUNK), dtype=bool)), w_spatial.astype(f32), 0.0)
        sv = jnp.einsum('hts,bnshc->bnthc', w_s, vg) + b_spatial.astype(f32).T[None, None, :, :, None]
        y_g = u * sv.reshape(B, S, D_G) * jax.nn.silu(z_g)

        y = jnp.concatenate([y_m, y_g], axis=-1).astype(x.dtype)
        h = h + jnp.einsum('bse,ed->bsd', y, w_out)
    return _rmsnorm(h, final_norm_w)
```

```python
import functools
import math

import jax
import jax.numpy as jnp
from jax import lax
from jax.experimental import pallas as pl
from jax.experimental.pallas import tpu as pltpu

F32 = jnp.float32
BF16 = jnp.bfloat16

N_HEADS = 8
QK = 128
DV = 256
CHUNK = 128
CONV_W = 4
EPS = 1e-6
GATE_PAD = 128
CARRY = 8

C_QK, C_V, C_O, C_Z, C_U, C_VG, C_ZG = 0, 256, 512, 768, 1024, 1280, 1536
HEAD_COLS = 1792

SEQ_BLOCK = 512
VMEM_LIMIT_BYTES = 56 * 1024 * 1024


def _sigmoid(x):
    return 1.0 / (1.0 + jnp.exp(-x))


def _silu(x):
    return x * _sigmoid(x)


def _gelu_tanh(x):
    c = math.sqrt(2.0 / math.pi)
    return 0.5 * x * (1.0 + jnp.tanh(c * (x + 0.044715 * (x * x * x))))


def _log_sigmoid(x):
    return jnp.minimum(x, 0.0) - jnp.log1p(jnp.exp(-jnp.abs(x)))


def _layer_kernel(x_ref, norm_w_ref, w_in_ref, w_g_ref, gate_b_ref, conv_w_ref, conv_b_ref,
                  mnorm_w_ref, sgu_w_ref, sgu_b_ref, w_sp_ref, b_sp_ref, w_out_ref, fnorm_w_ref,
                  out_ref,
                  xn_scr, proj_scr, g_scr, lf_scr, c_scr, n_scr, m_scr, carry_scr, y_scr,
                  *, tb):
    s_blk = pl.program_id(1)
    h = pl.program_id(2)
    n_chunks = tb // CHUNK

    row_id = lax.broadcasted_iota(jnp.int32, (CHUNK, CHUNK), 0)
    col_id = lax.broadcasted_iota(jnp.int32, (CHUNK, CHUNK), 1)
    causal = col_id <= row_id

    @pl.when(jnp.logical_and(s_blk == 0, h == 0))
    def _reset_state():
        c_scr[...] = jnp.zeros_like(c_scr)
        n_scr[...] = jnp.zeros_like(n_scr)
        m_scr[...] = jnp.zeros_like(m_scr)
        carry_scr[...] = jnp.zeros_like(carry_scr)

    @pl.when(h == 0)
    def _block_prologue():
        x = x_ref[...]
        ms = jnp.mean(x * x, axis=-1, keepdims=True)
        xn = x * lax.rsqrt(ms + EPS) * norm_w_ref[...]
        xn_bf = xn.astype(BF16)
        xn_scr[...] = xn_bf
        out_ref[...] = x
        gates = jnp.dot(xn_bf, w_g_ref[...], preferred_element_type=F32)
        upper = (row_id <= col_id).astype(F32)
        gb = gate_b_ref[...]
        for c in range(n_chunks):
            gt = gates[c * CHUNK:(c + 1) * CHUNK, :].T
            li = gt[0:N_HEADS, :] + gb[0:N_HEADS, :]
            lf = _log_sigmoid(gt[N_HEADS:2 * N_HEADS, :] + gb[N_HEADS:2 * N_HEADS, :])
            b = jnp.dot(lf, upper, preferred_element_type=F32,
                        precision=lax.Precision.HIGHEST)
            g_scr[c] = li - b
            lf_scr[c] = lf

    proj_scr[0:CARRY, 0:2 * QK] = carry_scr[h]
    proj_scr[CARRY:CARRY + tb, :] = jnp.dot(xn_scr[...], w_in_ref[...],
                                            preferred_element_type=F32)
    carry_scr[h] = proj_scr[tb:tb + CARRY, 0:2 * QK]

    conv_w = conv_w_ref[...]
    conv_b = conv_b_ref[...]
    mnorm_w = mnorm_w_ref[...]
    sgu_w = sgu_w_ref[...]
    sgu_b = sgu_b_ref[...]
    w_sp = jnp.where(causal, w_sp_ref[...], 0.0).astype(BF16)
    b_sp = b_sp_ref[...]
    k_scale = QK ** -0.5

    c_state = c_scr[h]
    n_row = n_scr[h][0:1, :]
    m11 = m_scr[h][0:1, 0:1]

    for c in range(n_chunks):
        r0 = CARRY + c * CHUNK
        rows = pl.ds(r0, CHUNK)

        qk_pre = conv_b
        for j in range(CONV_W):
            qk_pre = qk_pre + conv_w[j:j + 1, :] * proj_scr[pl.ds(r0 - (CONV_W - 1) + j, CHUNK),
                                                             C_QK:C_QK + 2 * QK]
        qk_act = _silu(qk_pre)
        q = qk_act[:, 0:QK]
        k = qk_act[:, QK:2 * QK] * k_scale
        q_bf = q.astype(BF16)
        k_bf = k.astype(BF16)
        v_bf = proj_scr[rows, C_V:C_V + DV].astype(BF16)

        g_row = g_scr[c, pl.ds(h, 1), :]
        lf_row = lf_scr[c, pl.ds(h, 1), :]
        mx = jnp.maximum(jnp.max(jnp.where(causal, g_row, -jnp.inf), axis=1, keepdims=True), m11)
        b_col = jnp.sum(jnp.where(causal, lf_row, 0.0), axis=1, keepdims=True)
        p = jnp.exp(jnp.where(causal, g_row - mx, -jnp.inf))
        decay = jnp.exp(m11 - mx)

        s_qk = lax.dot_general(q_bf, k_bf, (((1,), (1,)), ((), ())),
                               preferred_element_type=F32) * p
        num = (jnp.dot(s_qk.astype(BF16), v_bf, preferred_element_type=F32)
               + decay * jnp.dot(q_bf, c_state.astype(BF16), preferred_element_type=F32))
        den = (jnp.sum(s_qk, axis=1, keepdims=True)
               + decay * jnp.sum(q * n_row, axis=1, keepdims=True))
        hm = num / jnp.maximum(jnp.abs(den), jnp.exp(-(b_col + mx)))

        ws_row = p[CHUNK - 1:CHUNK, :]
        a11 = decay[CHUNK - 1:CHUNK, :]
        kt_w = (k.T * ws_row).astype(BF16)
        c_state = a11 * c_state + jnp.dot(kt_w, v_bf, preferred_element_type=F32)
        ws8 = jnp.broadcast_to(ws_row, (8, CHUNK)).astype(BF16)
        n_row = a11 * n_row + jnp.dot(ws8, k_bf, preferred_element_type=F32)[0:1, :]
        m11 = b_col[CHUNK - 1:CHUNK, :] + mx[CHUNK - 1:CHUNK, :]

        hm = hm * lax.rsqrt(jnp.mean(hm * hm, axis=-1, keepdims=True) + EPS) * mnorm_w
        y_m = hm * _sigmoid(proj_scr[rows, C_O:C_O + DV]) * _silu(proj_scr[rows, C_Z:C_Z + DV])
        y_scr[c * CHUNK:(c + 1) * CHUNK, 0:DV] = y_m.astype(BF16)

        gv = _gelu_tanh(proj_scr[rows, C_VG:C_VG + DV])
        mu = jnp.mean(gv, axis=-1, keepdims=True)
        gc = gv - mu
        var = jnp.mean(gc * gc, axis=-1, keepdims=True)
        vgn = gc * lax.rsqrt(var + EPS) * sgu_w + sgu_b
        sv = jnp.dot(w_sp, vgn.astype(BF16), preferred_element_type=F32) + b_sp
        y_g = _gelu_tanh(proj_scr[rows, C_U:C_U + DV]) * sv * _silu(proj_scr[rows, C_ZG:C_ZG + DV])
        y_scr[c * CHUNK:(c + 1) * CHUNK, DV:2 * DV] = y_g.astype(BF16)

    c_scr[h] = c_state
    n_scr[h] = jnp.broadcast_to(n_row, (8, QK))
    m_scr[h] = jnp.broadcast_to(m11, (8, QK))

    out_ref[...] += jnp.dot(y_scr[...], w_out_ref[...], preferred_element_type=F32)

    @pl.when(h == N_HEADS - 1)
    def _block_epilogue():
        r = out_ref[...]
        ms = jnp.mean(r * r, axis=-1, keepdims=True)
        out_ref[...] = r * lax.rsqrt(ms + EPS) * fnorm_w_ref[...]


def kernel(x, norm_w, w_in, conv_w, conv_b, b_igate, b_fgate, mlstm_norm_w, sgu_norm_w,
           sgu_norm_b, w_spatial, b_spatial, w_out, final_norm_w):
    B, S, D = x.shape
    H = N_HEADS
    d_qk = H * QK
    d_m = H * DV
    assert D == d_m and w_in.shape == (D, 2 * d_qk + 3 * d_m + 2 * H + 3 * d_m)
    assert w_out.shape == (2 * d_m, D) and w_spatial.shape == (H, CHUNK, CHUNK)
    tb = SEQ_BLOCK
    assert S % tb == 0 and tb % CHUNK == 0

    o_q, o_k, o_v, o_o, o_z = 0, d_qk, 2 * d_qk, 2 * d_qk + d_m, 2 * d_qk + 2 * d_m
    o_i = 2 * d_qk + 3 * d_m
    o_u = o_i + 2 * H
    o_vg, o_zg = o_u + d_m, o_u + 2 * d_m

    def per_head(off, width):
        return w_in[:, off:off + H * width].reshape(D, H, width)

    w_in_h = jnp.concatenate(
        [per_head(o_q, QK), per_head(o_k, QK), per_head(o_v, DV), per_head(o_o, DV),
         per_head(o_z, DV), per_head(o_u, DV), per_head(o_vg, DV), per_head(o_zg, DV)],
        axis=-1).transpose(1, 0, 2).astype(BF16)
    w_g = jnp.pad(w_in[:, o_i:o_i + 2 * H], ((0, 0), (0, GATE_PAD - 2 * H))).astype(BF16)
    gate_b = jnp.concatenate([b_igate, b_fgate]).astype(F32).reshape(2 * H, 1)
    conv_w_h = jnp.concatenate([conv_w[:, :d_qk].reshape(CONV_W, H, QK),
                                conv_w[:, d_qk:].reshape(CONV_W, H, QK)],
                               axis=-1).transpose(1, 0, 2).astype(F32)
    conv_b_h = jnp.concatenate([conv_b[:d_qk].reshape(H, 1, QK),
                                conv_b[d_qk:].reshape(H, 1, QK)], axis=-1).astype(F32)
    w_out_h = jnp.concatenate([w_out[:d_m].reshape(H, DV, D),
                               w_out[d_m:].reshape(H, DV, D)], axis=1).astype(BF16)

    grid = (B, S // tb, H)
    head3 = lambda b, s, h: (h, 0, 0)
    fixed2 = lambda b, s, h: (0, 0)
    tok3 = lambda b, s, h: (b, s, 0)

    return pl.pallas_call(
        functools.partial(_layer_kernel, tb=tb),
        out_shape=jax.ShapeDtypeStruct((B, S, D), x.dtype),
        grid=grid,
        in_specs=[
            pl.BlockSpec((None, tb, D), tok3),
            pl.BlockSpec((1, D), fixed2),
            pl.BlockSpec((None, D, HEAD_COLS), head3),
            pl.BlockSpec((D, GATE_PAD), fixed2),
            pl.BlockSpec((2 * H, 1), fixed2),
            pl.BlockSpec((None, CONV_W, 2 * QK), head3),
            pl.BlockSpec((None, 1, 2 * QK), head3),
            pl.BlockSpec((None, 1, DV), head3),
            pl.BlockSpec((None, 1, DV), head3),
            pl.BlockSpec((None, 1, DV), head3),
            pl.BlockSpec((None, CHUNK, CHUNK), head3),
            pl.BlockSpec((None, CHUNK, 1), head3),
            pl.BlockSpec((None, 2 * DV, D), head3),
            pl.BlockSpec((1, D), fixed2),
        ],
        out_specs=pl.BlockSpec((None, tb, D), tok3),
        scratch_shapes=[
            pltpu.VMEM((tb, D), BF16),
            pltpu.VMEM((CARRY + tb, HEAD_COLS), F32),
            pltpu.VMEM((tb // CHUNK, H, CHUNK), F32),
            pltpu.VMEM((tb // CHUNK, H, CHUNK), F32),
            pltpu.VMEM((H, QK, DV), F32),
            pltpu.VMEM((H, 8, QK), F32),
            pltpu.VMEM((H, 8, QK), F32),
            pltpu.VMEM((H, CARRY, 2 * QK), F32),
            pltpu.VMEM((tb, 2 * DV), BF16),
        ],
        compiler_params=pltpu.CompilerParams(
            dimension_semantics=("arbitrary", "arbitrary", "arbitrary"),
            vmem_limit_bytes=VMEM_LIMIT_BYTES),
        name="hybrid_layer",
    )(x, norm_w.reshape(1, D).astype(F32), w_in_h, w_g, gate_b, conv_w_h, conv_b_h,
      mlstm_norm_w.reshape(H, 1, DV).astype(F32), sgu_norm_w.reshape(H, 1, DV).astype(F32),
      sgu_norm_b.reshape(H, 1, DV).astype(F32), w_spatial.astype(F32),
      b_spatial.reshape(H, CHUNK, 1).astype(F32), w_out_h,
      final_norm_w.reshape(1, D).astype(F32))
```

```python
import functools
import math

import jax
import jax.numpy as jnp
from jax import lax
from jax.experimental import pallas as pl
from jax.experimental.pallas import tpu as pltpu

F32 = jnp.float32
BF16 = jnp.bfloat16

N_HEADS = 8
QK = 128
DV = 256
CHUNK = 128
CONV_W = 4
EPS = 1e-6
GATE_PAD = 128
CARRY = 8

C_QK, C_V, C_O, C_Z, C_U, C_VG, C_ZG = 0, 256, 512, 768, 1024, 1280, 1536
HEAD_COLS = 1792
PIECE_COLS = 256

SEQ_BLOCK = 512
VMEM_LIMIT_BYTES = 56 * 1024 * 1024


def _sigmoid(x):
    return 1.0 / (1.0 + jnp.exp(-x))


def _silu(x):
    return x * _sigmoid(x)


def _gelu_tanh(x):
    c = math.sqrt(2.0 / math.pi)
    return 0.5 * x * (1.0 + jnp.tanh(c * (x + 0.044715 * (x * x * x))))


def _log_sigmoid(x):
    return jnp.minimum(x, 0.0) - jnp.log1p(jnp.exp(-jnp.abs(x)))


def _layer_kernel(x_ref, norm_w_ref, w_in_ref, w_g_ref, gate_b_ref, conv_w_ref, conv_b_ref,
                  mnorm_w_ref, sgu_w_ref, sgu_b_ref, w_sp_ref, b_sp_ref, w_out_ref, fnorm_w_ref,
                  out_ref,
                  xn_scr, proj_scr, g_scr, lf_scr, c_scr, n_scr, m_scr, carry_scr, y_scr,
                  *, tb):
    s_blk = pl.program_id(1)
    h = pl.program_id(2)
    n_chunks = tb // CHUNK

    row_id = lax.broadcasted_iota(jnp.int32, (CHUNK, CHUNK), 0)
    col_id = lax.broadcasted_iota(jnp.int32, (CHUNK, CHUNK), 1)
    causal = col_id <= row_id

    @pl.when(jnp.logical_and(s_blk == 0, h == 0))
    def _reset_state():
        c_scr[...] = jnp.zeros_like(c_scr)
        n_scr[...] = jnp.zeros_like(n_scr)
        m_scr[...] = jnp.zeros_like(m_scr)
        carry_scr[...] = jnp.zeros_like(carry_scr)

    @pl.when(h == 0)
    def _block_prologue():
        x = x_ref[...]
        ms = jnp.mean(x * x, axis=-1, keepdims=True)
        xn = x * lax.rsqrt(ms + EPS) * norm_w_ref[...]
        xn_bf = xn.astype(BF16)
        xn_scr[...] = xn_bf
        out_ref[...] = x
        gates = jnp.dot(xn_bf, w_g_ref[...], preferred_element_type=F32)
        upper = (row_id <= col_id).astype(F32)
        gb = gate_b_ref[...]
        for c in range(n_chunks):
            gt = gates[c * CHUNK:(c + 1) * CHUNK, :].T
            li = gt[0:N_HEADS, :] + gb[0:N_HEADS, :]
            lf = _log_sigmoid(gt[N_HEADS:2 * N_HEADS, :] + gb[N_HEADS:2 * N_HEADS, :])
            b = jnp.dot(lf, upper, preferred_element_type=F32,
                        precision=lax.Precision.HIGHEST)
            g_scr[c] = li - b
            lf_scr[c] = lf

    proj_scr[0:CARRY, 0:2 * QK] = carry_scr[h]

    conv_w = conv_w_ref[...]
    conv_b = conv_b_ref[...]
    mnorm_w = mnorm_w_ref[...]
    sgu_w = sgu_w_ref[...]
    sgu_b = sgu_b_ref[...]
    w_sp = jnp.where(causal, w_sp_ref[...], 0.0).astype(BF16)
    b_sp = b_sp_ref[...]
    k_scale = QK ** -0.5

    c_state = c_scr[h]
    n_row = n_scr[h][0:1, :]
    m11 = m_scr[h][0:1, 0:1]

    n_pieces = HEAD_COLS // PIECE_COLS

    def project_piece(c, j):
        cols = slice(j * PIECE_COLS, (j + 1) * PIECE_COLS)
        proj_scr[pl.ds(CARRY + c * CHUNK, CHUNK), cols] = jnp.dot(
            xn_scr[pl.ds(c * CHUNK, CHUNK), :], w_in_ref[:, cols], preferred_element_type=F32)

    n_out_pieces = out_ref.shape[-1] // PIECE_COLS

    def outproj_piece(row0, n_rows, jo):
        cols = slice(jo * PIECE_COLS, (jo + 1) * PIECE_COLS)
        out_ref[pl.ds(row0, n_rows), cols] += jnp.dot(
            y_scr[pl.ds(row0, n_rows), :], w_out_ref[:, cols], preferred_element_type=F32)

    for j in range(n_pieces):
        project_piece(0, j)
    for c in range(n_chunks):
        r0 = CARRY + c * CHUNK
        rows = pl.ds(r0, CHUNK)
        tok = pl.ds(c * CHUNK, CHUNK)

        def next_piece(j):
            if c + 1 < n_chunks:
                project_piece(c + 1, j)
            else:
                for jo in range(j * n_out_pieces // n_pieces, (j + 1) * n_out_pieces // n_pieces):
                    outproj_piece(0, (n_chunks - 1) * CHUNK, jo)

        next_piece(0)
        qk_pre = conv_b
        for j in range(CONV_W):
            qk_pre = qk_pre + conv_w[j:j + 1, :] * proj_scr[pl.ds(r0 - (CONV_W - 1) + j, CHUNK),
                                                             C_QK:C_QK + 2 * QK]
        qk_act = _silu(qk_pre)
        q = qk_act[:, 0:QK]
        k = qk_act[:, QK:2 * QK] * k_scale
        q_bf = q.astype(BF16)
        k_bf = k.astype(BF16)
        v_bf = proj_scr[rows, C_V:C_V + DV].astype(BF16)

        g_row = g_scr[c, pl.ds(h, 1), :]
        lf_row = lf_scr[c, pl.ds(h, 1), :]
        mx = jnp.maximum(jnp.max(jnp.where(causal, g_row, -jnp.inf), axis=1, keepdims=True), m11)
        b_col = jnp.sum(jnp.where(causal, lf_row, 0.0), axis=1, keepdims=True)
        p = jnp.exp(jnp.where(causal, g_row - mx, -jnp.inf))
        decay = jnp.exp(m11 - mx)

        s_raw = lax.dot_general(q_bf, k_bf, (((1,), (1,)), ((), ())),
                                preferred_element_type=F32)
        q_c = jnp.dot(q_bf, c_state.astype(BF16), preferred_element_type=F32)
        next_piece(1)

        s_qk = s_raw * p
        ws_row = p[CHUNK - 1:CHUNK, :]
        a11 = decay[CHUNK - 1:CHUNK, :]
        kt_w = (k.T * ws_row).astype(BF16)
        ws8 = jnp.broadcast_to(ws_row, (8, CHUNK)).astype(BF16)
        num = jnp.dot(s_qk.astype(BF16), v_bf, preferred_element_type=F32) + decay * q_c
        c_state = a11 * c_state + jnp.dot(kt_w, v_bf, preferred_element_type=F32)
        n_new = a11 * n_row + jnp.dot(ws8, k_bf, preferred_element_type=F32)[0:1, :]
        next_piece(2)

        den = (jnp.sum(s_qk, axis=1, keepdims=True)
               + decay * jnp.sum(q * n_row, axis=1, keepdims=True))
        n_row = n_new
        hm = num / jnp.maximum(jnp.abs(den), jnp.exp(-(b_col + mx)))
        m11 = b_col[CHUNK - 1:CHUNK, :] + mx[CHUNK - 1:CHUNK, :]
        hm = hm * lax.rsqrt(jnp.mean(hm * hm, axis=-1, keepdims=True) + EPS) * mnorm_w
        y_m = hm * _sigmoid(proj_scr[rows, C_O:C_O + DV]) * _silu(proj_scr[rows, C_Z:C_Z + DV])
        y_scr[tok, 0:DV] = y_m.astype(BF16)
        next_piece(3)

        gv = _gelu_tanh(proj_scr[rows, C_VG:C_VG + DV])
        mu = jnp.mean(gv, axis=-1, keepdims=True)
        gc = gv - mu
        var = jnp.mean(gc * gc, axis=-1, keepdims=True)
        vgn = gc * lax.rsqrt(var + EPS) * sgu_w + sgu_b
        sv = jnp.dot(w_sp, vgn.astype(BF16), preferred_element_type=F32) + b_sp
        next_piece(4)
        y_g = _gelu_tanh(proj_scr[rows, C_U:C_U + DV]) * sv * _silu(proj_scr[rows, C_ZG:C_ZG + DV])
        y_scr[tok, DV:2 * DV] = y_g.astype(BF16)
        next_piece(5)
        next_piece(6)
    for jo in range(n_out_pieces):
        outproj_piece((n_chunks - 1) * CHUNK, CHUNK, jo)

    c_scr[h] = c_state
    n_scr[h] = jnp.broadcast_to(n_row, (8, QK))
    m_scr[h] = jnp.broadcast_to(m11, (8, QK))

    carry_scr[h] = proj_scr[tb:tb + CARRY, 0:2 * QK]

    @pl.when(h == N_HEADS - 1)
    def _block_epilogue():
        r = out_ref[...]
        ms = jnp.mean(r * r, axis=-1, keepdims=True)
        out_ref[...] = r * lax.rsqrt(ms + EPS) * fnorm_w_ref[...]


def kernel(x, norm_w, w_in, conv_w, conv_b, b_igate, b_fgate, mlstm_norm_w, sgu_norm_w,
           sgu_norm_b, w_spatial, b_spatial, w_out, final_norm_w):
    B, S, D = x.shape
    H = N_HEADS
    d_qk = H * QK
    d_m = H * DV
    assert D == d_m and w_in.shape == (D, 2 * d_qk + 3 * d_m + 2 * H + 3 * d_m)
    assert w_out.shape == (2 * d_m, D) and w_spatial.shape == (H, CHUNK, CHUNK)
    tb = SEQ_BLOCK
    assert S % tb == 0 and tb % CHUNK == 0

    o_q, o_k, o_v, o_o, o_z = 0, d_qk, 2 * d_qk, 2 * d_qk + d_m, 2 * d_qk + 2 * d_m
    o_i = 2 * d_qk + 3 * d_m
    o_u = o_i + 2 * H
    o_vg, o_zg = o_u + d_m, o_u + 2 * d_m

    def per_head(off, width):
        return w_in[:, off:off + H * width].reshape(D, H, width)

    w_in_h = jnp.concatenate(
        [per_head(o_q, QK), per_head(o_k, QK), per_head(o_v, DV), per_head(o_o, DV),
         per_head(o_z, DV), per_head(o_u, DV), per_head(o_vg, DV), per_head(o_zg, DV)],
        axis=-1).astype(BF16).reshape(D, H * HEAD_COLS)
    w_g = jnp.pad(w_in[:, o_i:o_i + 2 * H], ((0, 0), (0, GATE_PAD - 2 * H))).astype(BF16)
    gate_b = jnp.concatenate([b_igate, b_fgate]).astype(F32).reshape(2 * H, 1)
    conv_w_h = jnp.concatenate([conv_w[:, :d_qk].reshape(CONV_W, H, QK),
                                conv_w[:, d_qk:].reshape(CONV_W, H, QK)],
                               axis=-1).transpose(1, 0, 2).astype(F32)
    conv_b_h = jnp.concatenate([conv_b[:d_qk].reshape(H, 1, QK),
                                conv_b[d_qk:].reshape(H, 1, QK)], axis=-1).astype(F32)
    w_out_h = jnp.concatenate([w_out[:d_m].reshape(H, DV, D),
                               w_out[d_m:].reshape(H, DV, D)], axis=1).astype(BF16)

    grid = (B, S // tb, H)
    head3 = lambda b, s, h: (h, 0, 0)
    fixed2 = lambda b, s, h: (0, 0)
    tok3 = lambda b, s, h: (b, s, 0)

    return pl.pallas_call(
        functools.partial(_layer_kernel, tb=tb),
        out_shape=jax.ShapeDtypeStruct((B, S, D), x.dtype),
        grid=grid,
        in_specs=[
            pl.BlockSpec((None, tb, D), tok3),
            pl.BlockSpec((1, D), fixed2),
            pl.BlockSpec((D, HEAD_COLS), lambda b, s, h: (0, h)),
            pl.BlockSpec((D, GATE_PAD), fixed2),
            pl.BlockSpec((2 * H, 1), fixed2),
            pl.BlockSpec((None, CONV_W, 2 * QK), head3),
            pl.BlockSpec((None, 1, 2 * QK), head3),
            pl.BlockSpec((None, 1, DV), head3),
            pl.BlockSpec((None, 1, DV), head3),
            pl.BlockSpec((None, 1, DV), head3),
            pl.BlockSpec((None, CHUNK, CHUNK), head3),
            pl.BlockSpec((None, CHUNK, 1), head3),
            pl.BlockSpec((None, 2 * DV, D), head3),
            pl.BlockSpec((1, D), fixed2),
        ],
        out_specs=pl.BlockSpec((None, tb, D), tok3),
        scratch_shapes=[
            pltpu.VMEM((tb, D), BF16),
            pltpu.VMEM((CARRY + tb, HEAD_COLS), F32),
            pltpu.VMEM((tb // CHUNK, H, CHUNK), F32),
            pltpu.VMEM((tb // CHUNK, H, CHUNK), F32),
            pltpu.VMEM((H, QK, DV), F32),
            pltpu.VMEM((H, 8, QK), F32),
            pltpu.VMEM((H, 8, QK), F32),
            pltpu.VMEM((H, CARRY, 2 * QK), F32),
            pltpu.VMEM((tb, 2 * DV), BF16),
        ],
        compiler_params=pltpu.CompilerParams(
            dimension_semantics=("arbitrary", "arbitrary", "arbitrary"),
            vmem_limit_bytes=VMEM_LIMIT_BYTES),
        name="hybrid_layer",
    )(x, norm_w.reshape(1, D).astype(F32), w_in_h, w_g, gate_b, conv_w_h, conv_b_h,
      mlstm_norm_w.reshape(H, 1, DV).astype(F32), sgu_norm_w.reshape(H, 1, DV).astype(F32),
      sgu_norm_b.reshape(H, 1, DV).astype(F32), w_spatial.astype(F32),
      b_spatial.reshape(H, CHUNK, 1).astype(F32), w_out_h,
      final_norm_w.reshape(1, D).astype(F32))
```

```python
import functools
import math

import jax
import jax.numpy as jnp
from jax import lax
from jax.experimental import pallas as pl
from jax.experimental.pallas import tpu as pltpu

F32 = jnp.float32
BF16 = jnp.bfloat16

N_HEADS = 8
QK = 128
DV = 256
CHUNK = 128
CONV_W = 4
EPS = 1e-6
GATE_PAD = 128
CARRY = 8

C_QK, C_V, C_O, C_Z, C_U, C_VG, C_ZG = 0, 256, 512, 768, 1024, 1280, 1536
HEAD_COLS = 1792
PIECE_COLS = 256
N_W_COPIES = 10

SEQ_BLOCK = 512
VMEM_LIMIT_BYTES = 58 * 1024 * 1024


def _sigmoid(x):
    return 1.0 / (1.0 + jnp.exp(-x))


def _silu(x):
    return x * _sigmoid(x)


def _gelu_tanh(x):
    c = math.sqrt(2.0 / math.pi)
    return 0.5 * x * (1.0 + jnp.tanh(c * (x + 0.044715 * (x * x * x))))


def _log_sigmoid(x):
    return jnp.minimum(x, 0.0) - jnp.log1p(jnp.exp(-jnp.abs(x)))


def _aligned(v, m):
    if isinstance(v, int):
        assert v % m == 0
        return v
    return pl.multiple_of(v, m)


def _lane_cumsum(x):
    lane = lax.broadcasted_iota(jnp.int32, x.shape, 1)
    k = 1
    while k < x.shape[-1]:
        x = x + jnp.where(lane >= k, pltpu.roll(x, k, 1), 0.0)
        k *= 2
    return x


def _layer_kernel(x_ref, norm_w_ref, w_g_ref, gate_b_ref, conv_w_ref, conv_b_ref, mnorm_w_ref,
                  sgu_w_ref, sgu_b_ref, w_sp_ref, b_sp_ref, fnorm_w_ref,
                  w_a_hbm, w_b_hbm, w_out_hbm,
                  out_ref,
                  xres_scr, xn_scr, gates_scr, g_scr, lf_scr, xn_stage, g_stage, lf_stage,
                  w_in_scr, w_out_scr, w_sem, proj_scr, c_scr, n_scr, m_scr, carry_scr, y_scr,
                  *, tb, n_seq_blocks, n_blocks, col_offsets):
    i = pl.program_id(0)
    n_chunks = tb // CHUNK
    rows_per_head = tb // N_HEADS
    n_pieces = HEAD_COLS // PIECE_COLS
    n_out_pieces = out_ref.shape[-1] // PIECE_COLS
    d_m = N_HEADS * DV
    o_q, o_k, o_v, o_o, o_z = col_offsets
    nxt = i % 2
    cur = 1 - nxt

    row_id = lax.broadcasted_iota(jnp.int32, (CHUNK, CHUNK), 0)
    col_id = lax.broadcasted_iota(jnp.int32, (CHUNK, CHUNK), 1)
    causal = col_id <= row_id

    def weight_copies(head, slot):
        srcs = ((w_a_hbm, o_q + head * QK, QK, C_QK), (w_a_hbm, o_k + head * QK, QK, C_QK + QK),
                (w_a_hbm, o_v + head * DV, DV, C_V), (w_a_hbm, o_o + head * DV, DV, C_O),
                (w_a_hbm, o_z + head * DV, DV, C_Z), (w_b_hbm, head * DV, DV, C_U),
                (w_b_hbm, d_m + head * DV, DV, C_VG), (w_b_hbm, 2 * d_m + head * DV, DV, C_ZG))
        copies = [
            pltpu.make_async_copy(src.at[:, pl.ds(_aligned(col, QK), width)],
                                  w_in_scr.at[slot, :, pl.ds(dst, width)], w_sem.at[slot, n])
            for n, (src, col, width, dst) in enumerate(srcs)]
        for n, row in enumerate((head * DV, d_m + head * DV)):
            copies.append(pltpu.make_async_copy(
                w_out_hbm.at[pl.ds(_aligned(row, DV), DV), :],
                w_out_scr.at[slot, pl.ds(n * DV, DV), :], w_sem.at[slot, len(srcs) + n]))
        assert len(copies) == N_W_COPIES
        return copies

    def preprocess_norm(hh):
        rs = pl.ds(_aligned(hh * rows_per_head, rows_per_head), rows_per_head)
        xr = x_ref[rs, :]
        xres_scr[rs, :] = xr
        ms = jnp.mean(xr * xr, axis=-1, keepdims=True)
        xn_stage[...] = (xr * lax.rsqrt(ms + EPS) * norm_w_ref[...]).astype(BF16)

    def preprocess_gates(hh):
        rs = pl.ds(_aligned(hh * rows_per_head, rows_per_head), rows_per_head)
        gates_scr[rs, :] = jnp.dot(xn_stage[...], w_g_ref[...], preferred_element_type=F32)
        c2 = hh // (CHUNK // rows_per_head)
        gt = gates_scr[pl.ds(_aligned(c2 * CHUNK, CHUNK), CHUNK), :].T
        gb = gate_b_ref[...]
        li = gt[0:N_HEADS, :] + gb[0:N_HEADS, :]
        lf = _log_sigmoid(gt[N_HEADS:2 * N_HEADS, :] + gb[N_HEADS:2 * N_HEADS, :])
        g_stage[...] = li - _lane_cumsum(lf)
        lf_stage[...] = lf

    def commit_rows(hh):
        rs = pl.ds(_aligned(hh * rows_per_head, rows_per_head), rows_per_head)
        c2 = hh // (CHUNK // rows_per_head)
        xn_scr[nxt, rs, :] = xn_stage[...]
        g_scr[nxt, c2] = g_stage[...]
        lf_scr[nxt, c2] = lf_stage[...]

    def project_piece(xslot, c, j, wslot):
        cols = slice(j * PIECE_COLS, (j + 1) * PIECE_COLS)
        proj_scr[pl.ds(CARRY + c * CHUNK, CHUNK), cols] = jnp.dot(
            xn_scr[xslot, pl.ds(c * CHUNK, CHUNK), :], w_in_scr[wslot, :, cols],
            preferred_element_type=F32)

    def outproj_piece(wslot, row0, n_rows, jo):
        cols = slice(jo * PIECE_COLS, (jo + 1) * PIECE_COLS)
        out_ref[pl.ds(row0, n_rows), cols] += jnp.dot(
            y_scr[pl.ds(row0, n_rows), :], w_out_scr[wslot, :, cols], preferred_element_type=F32)

    def head_step(h, slot):
        for cp in weight_copies(h, slot):
            cp.wait()

        @pl.when(jnp.logical_or(h < N_HEADS - 1, i < n_blocks))
        def _prefetch():
            for cp in weight_copies((h + 1) % N_HEADS, 1 - slot):
                cp.start()

        proj_scr[0:CARRY, 0:2 * QK] = carry_scr[h]
        preprocess_norm(h)
        for j in range(n_pieces):
            project_piece(cur, 0, j, slot)

        conv_w = conv_w_ref[h]
        conv_b = conv_b_ref[h]
        mnorm_w = mnorm_w_ref[h]
        sgu_w = sgu_w_ref[h]
        sgu_b = sgu_b_ref[h]
        w_sp = jnp.where(causal, w_sp_ref[h], 0.0).astype(BF16)
        b_sp = b_sp_ref[h]
        k_scale = QK ** -0.5

        c_state = c_scr[h]
        n_row = n_scr[h][0:1, :]
        m11 = m_scr[h][0:1, 0:1]
        for c in range(n_chunks):
            src = proj_scr
            r0 = CARRY + c * CHUNK
            rows = pl.ds(r0, CHUNK)
            tok = pl.ds(c * CHUNK, CHUNK)

            def next_piece(j):
                if c == 1 and j == 0:
                    preprocess_gates(h)
                if c + 1 < n_chunks:
                    project_piece(cur, c + 1, j, slot)
                else:
                    for jo in range(j * n_out_pieces // n_pieces,
                                    (j + 1) * n_out_pieces // n_pieces):
                        outproj_piece(slot, 0, (n_chunks - 1) * CHUNK, jo)

            next_piece(0)
            qk_pre = conv_b
            for j in range(CONV_W):
                qk_pre = qk_pre + conv_w[j:j + 1, :] * src[pl.ds(r0 - (CONV_W - 1) + j, CHUNK),
                                                           C_QK:C_QK + 2 * QK]
            qk_act = _silu(qk_pre)
            q = qk_act[:, 0:QK]
            k = qk_act[:, QK:2 * QK] * k_scale
            q_bf = q.astype(BF16)
            k_bf = k.astype(BF16)
            v_bf = src[rows, C_V:C_V + DV].astype(BF16)

            g_row = g_scr[cur, c, pl.ds(h, 1), :]
            lf_row = lf_scr[cur, c, pl.ds(h, 1), :]
            mx = jnp.maximum(jnp.max(jnp.where(causal, g_row, -jnp.inf), axis=1, keepdims=True),
                             m11)
            b_col = jnp.sum(jnp.where(causal, lf_row, 0.0), axis=1, keepdims=True)
            p = jnp.exp(jnp.where(causal, g_row - mx, -jnp.inf))
            decay = jnp.exp(m11 - mx)

            s_raw = lax.dot_general(q_bf, k_bf, (((1,), (1,)), ((), ())),
                                    preferred_element_type=F32)
            q_c = jnp.dot(q_bf, c_state.astype(BF16), preferred_element_type=F32)
            next_piece(1)

            s_qk = s_raw * p
            ws_row = p[CHUNK - 1:CHUNK, :]
            a11 = decay[CHUNK - 1:CHUNK, :]
            kt_w = (k.T * ws_row).astype(BF16)
            ws8 = jnp.broadcast_to(ws_row, (8, CHUNK)).astype(BF16)
            num = jnp.dot(s_qk.astype(BF16), v_bf, preferred_element_type=F32) + decay * q_c
            c_state = a11 * c_state + jnp.dot(kt_w, v_bf, preferred_element_type=F32)
            n_new = a11 * n_row + jnp.dot(ws8, k_bf, preferred_element_type=F32)[0:1, :]
            next_piece(2)

            den = (jnp.sum(s_qk, axis=1, keepdims=True)
                   + decay * jnp.sum(q * n_row, axis=1, keepdims=True))
            n_row = n_new
            hm = num / jnp.maximum(jnp.abs(den), jnp.exp(-(b_col + mx)))
            m11 = b_col[CHUNK - 1:CHUNK, :] + mx[CHUNK - 1:CHUNK, :]
            hm = hm * lax.rsqrt(jnp.mean(hm * hm, axis=-1, keepdims=True) + EPS) * mnorm_w
            y_m = hm * _sigmoid(src[rows, C_O:C_O + DV]) * _silu(src[rows, C_Z:C_Z + DV])
            y_scr[tok, 0:DV] = y_m.astype(BF16)
            next_piece(3)

            gv = _gelu_tanh(src[rows, C_VG:C_VG + DV])
            mu = jnp.mean(gv, axis=-1, keepdims=True)
            gc = gv - mu
            var = jnp.mean(gc * gc, axis=-1, keepdims=True)
            vgn = gc * lax.rsqrt(var + EPS) * sgu_w + sgu_b
            sv = jnp.dot(w_sp, vgn.astype(BF16), preferred_element_type=F32) + b_sp
            next_piece(4)
            y_g = _gelu_tanh(src[rows, C_U:C_U + DV]) * sv * _silu(src[rows, C_ZG:C_ZG + DV])
            y_scr[tok, DV:2 * DV] = y_g.astype(BF16)
            next_piece(5)
            next_piece(6)
        for jo in range(n_out_pieces):
            outproj_piece(slot, (n_chunks - 1) * CHUNK, CHUNK, jo)

        c_scr[h] = c_state
        n_scr[h] = jnp.broadcast_to(n_row, (8, QK))
        m_scr[h] = jnp.broadcast_to(m11, (8, QK))
        carry_scr[h] = proj_scr[tb:tb + CARRY, 0:2 * QK]
        commit_rows(h)

    def head_pair_step(p, loop_carry):
        head_step(2 * p, 0)
        head_step(2 * p + 1, 1)
        return loop_carry

    @pl.when(i == 0)
    def _lead_in():
        for cp in weight_copies(0, 0):
            cp.start()
        gates_scr[...] = jnp.zeros_like(gates_scr)
        for hh in range(N_HEADS):
            preprocess_norm(hh)
            preprocess_gates(hh)
            commit_rows(hh)

    @pl.when(i > 0)
    def _mix_block():
        @pl.when((i - 1) % n_seq_blocks == 0)
        def _reset_state():
            c_scr[...] = jnp.zeros_like(c_scr)
            n_scr[...] = jnp.zeros_like(n_scr)
            m_scr[...] = jnp.zeros_like(m_scr)
            carry_scr[...] = jnp.zeros_like(carry_scr)

        out_ref[...] = xres_scr[...]
        lax.fori_loop(0, N_HEADS // 2, head_pair_step, 0)
        r = out_ref[...]
        ms = jnp.mean(r * r, axis=-1, keepdims=True)
        out_ref[...] = r * lax.rsqrt(ms + EPS) * fnorm_w_ref[...]


def kernel(x, norm_w, w_in, conv_w, conv_b, b_igate, b_fgate, mlstm_norm_w, sgu_norm_w,
           sgu_norm_b, w_spatial, b_spatial, w_out, final_norm_w):
    B, S, D = x.shape
    H = N_HEADS
    d_qk = H * QK
    d_m = H * DV
    assert D == d_m and w_in.shape == (D, 2 * d_qk + 3 * d_m + 2 * H + 3 * d_m)
    assert w_out.shape == (2 * d_m, D) and w_spatial.shape == (H, CHUNK, CHUNK)
    tb = SEQ_BLOCK
    assert S % tb == 0 and tb % CHUNK == 0 and tb // CHUNK >= 2 and CHUNK % (tb // H) == 0
    n_seq_blocks = S // tb
    n_blocks = B * n_seq_blocks

    o_q, o_k, o_v, o_o, o_z = 0, d_qk, 2 * d_qk, 2 * d_qk + d_m, 2 * d_qk + 2 * d_m
    o_i = 2 * d_qk + 3 * d_m
    o_u = o_i + 2 * H
    assert o_i % GATE_PAD == 0 and all(o % QK == 0 for o in (o_q, o_k, o_v, o_o, o_z))

    w_bf = w_in.astype(BF16)
    w_b = w_bf[:, o_u:]
    w_g = w_bf[:, o_i:o_i + GATE_PAD]
    w_out_bf = w_out.astype(BF16)
    gate_b = jnp.concatenate([b_igate, b_fgate]).astype(F32).reshape(2 * H, 1)
    conv_w_h = jnp.concatenate([conv_w[:, :d_qk].reshape(CONV_W, H, QK),
                                conv_w[:, d_qk:].reshape(CONV_W, H, QK)],
                               axis=-1).transpose(1, 0, 2).astype(F32)
    conv_b_h = jnp.concatenate([conv_b[:d_qk].reshape(H, 1, QK),
                                conv_b[d_qk:].reshape(H, 1, QK)], axis=-1).astype(F32)

    def resident(shape):
        return pl.BlockSpec(shape, lambda i: (0,) * len(shape))

    hbm = pl.BlockSpec(memory_space=pl.ANY)
    n_chunks = tb // CHUNK
    out = pl.pallas_call(
        functools.partial(_layer_kernel, tb=tb, n_seq_blocks=n_seq_blocks, n_blocks=n_blocks,
                          col_offsets=(o_q, o_k, o_v, o_o, o_z)),
        out_shape=jax.ShapeDtypeStruct((B * S, D), x.dtype),
        grid=(n_blocks + 1,),
        in_specs=[
            pl.BlockSpec((tb, D), lambda i: (jnp.minimum(i, n_blocks - 1), 0)),
            resident((1, D)),
            resident((D, GATE_PAD)),
            resident((2 * H, 1)),
            resident((H, CONV_W, 2 * QK)),
            resident((H, 1, 2 * QK)),
            resident((H, 1, DV)),
            resident((H, 1, DV)),
            resident((H, 1, DV)),
            resident((H, CHUNK, CHUNK)),
            resident((H, CHUNK, 1)),
            resident((1, D)),
            hbm, hbm, hbm,
        ],
        out_specs=pl.BlockSpec((tb, D), lambda i: (jnp.maximum(i - 1, 0), 0)),
        scratch_shapes=[
            pltpu.VMEM((tb, D), F32),
            pltpu.VMEM((2, tb, D), BF16),
            pltpu.VMEM((tb, GATE_PAD), F32),
            pltpu.VMEM((2, n_chunks, H, CHUNK), F32),
            pltpu.VMEM((2, n_chunks, H, CHUNK), F32),
            pltpu.VMEM((tb // H, D), BF16),
            pltpu.VMEM((H, CHUNK), F32),
            pltpu.VMEM((H, CHUNK), F32),
            pltpu.VMEM((2, D, HEAD_COLS), BF16),
            pltpu.VMEM((2, 2 * DV, D), BF16),
            pltpu.SemaphoreType.DMA((2, N_W_COPIES)),
            pltpu.VMEM((CARRY + tb, HEAD_COLS), F32),
            pltpu.VMEM((H, QK, DV), F32),
            pltpu.VMEM((H, 8, QK), F32),
            pltpu.VMEM((H, 8, QK), F32),
            pltpu.VMEM((H, CARRY, 2 * QK), F32),
            pltpu.VMEM((tb, 2 * DV), BF16),
        ],
        compiler_params=pltpu.CompilerParams(
            dimension_semantics=("arbitrary",),
            vmem_limit_bytes=VMEM_LIMIT_BYTES),
        name="hybrid_layer",
    )(x.reshape(B * S, D), norm_w.reshape(1, D).astype(F32), w_g, gate_b, conv_w_h, conv_b_h,
      mlstm_norm_w.reshape(H, 1, DV).astype(F32), sgu_norm_w.reshape(H, 1, DV).astype(F32),
      sgu_norm_b.reshape(H, 1, DV).astype(F32), w_spatial.astype(F32),
      b_spatial.reshape(H, CHUNK, 1).astype(F32), final_norm_w.reshape(1, D).astype(F32),
      w_bf, w_b, w_out_bf)
    return out.reshape(B, S, D)
```

```python
import functools
import math

import jax
import jax.numpy as jnp
from jax import lax
from jax.experimental import pallas as pl
from jax.experimental.pallas import tpu as pltpu

F32 = jnp.float32
BF16 = jnp.bfloat16

N_HEADS = 8
QK = 128
DV = 256
CHUNK = 128
CONV_W = 4
EPS = 1e-6
GATE_PAD = 128
CARRY = 8

C_QK, C_V, C_O, C_Z, C_U, C_VG, C_ZG = 0, 256, 512, 768, 1024, 1280, 1536
HEAD_COLS = 1792
PIECE_COLS = 256

SEQ_BLOCK = 512
VMEM_LIMIT_BYTES = 56 * 1024 * 1024


def _sigmoid(x):
    return 1.0 / (1.0 + jnp.exp(-x))


def _silu(x):
    return x * _sigmoid(x)


def _gelu_tanh(x):
    c = math.sqrt(2.0 / math.pi)
    return 0.5 * x * (1.0 + jnp.tanh(c * (x + 0.044715 * (x * x * x))))


def _log_sigmoid(x):
    return jnp.minimum(x, 0.0) - jnp.log1p(jnp.exp(-jnp.abs(x)))


def _lane_cumsum(x):
    lane = lax.broadcasted_iota(jnp.int32, x.shape, 1)
    k = 1
    while k < x.shape[-1]:
        x = x + jnp.where(lane >= k, pltpu.roll(x, k, 1), 0.0)
        k *= 2
    return x


def _layer_kernel(x_ref, norm_w_ref, wq_ref, wk_ref, wv_ref, wo_ref, wz_ref, wu_ref, wvg_ref,
                  wzg_ref, w_g_ref, gate_b_ref, conv_w_ref, conv_b_ref, mnorm_w_ref, sgu_w_ref,
                  sgu_b_ref, w_sp_ref, b_sp_ref, w_out_m_ref, w_out_g_ref, fnorm_w_ref,
                  out_ref,
                  xn_scr, wqk_scr, proj_scr, g_scr, lf_scr, c_scr, n_scr, m_scr, carry_scr, y_scr,
                  *, tb):
    s_blk = pl.program_id(1)
    h = pl.program_id(2)
    n_chunks = tb // CHUNK

    row_id = lax.broadcasted_iota(jnp.int32, (CHUNK, CHUNK), 0)
    col_id = lax.broadcasted_iota(jnp.int32, (CHUNK, CHUNK), 1)
    causal = col_id <= row_id

    @pl.when(jnp.logical_and(s_blk == 0, h == 0))
    def _reset_state():
        c_scr[...] = jnp.zeros_like(c_scr)
        n_scr[...] = jnp.zeros_like(n_scr)
        m_scr[...] = jnp.zeros_like(m_scr)
        carry_scr[...] = jnp.zeros_like(carry_scr)

    @pl.when(h == 0)
    def _block_prologue():
        x = x_ref[...]
        ms = jnp.mean(x * x, axis=-1, keepdims=True)
        xn = x * lax.rsqrt(ms + EPS) * norm_w_ref[...]
        xn_bf = xn.astype(BF16)
        xn_scr[...] = xn_bf
        out_ref[...] = x
        gates = jnp.dot(xn_bf, w_g_ref[...], preferred_element_type=F32)
        gb = gate_b_ref[...]
        for c in range(n_chunks):
            gt = gates[c * CHUNK:(c + 1) * CHUNK, :].T
            li = gt[0:N_HEADS, :] + gb[0:N_HEADS, :]
            lf = _log_sigmoid(gt[N_HEADS:2 * N_HEADS, :] + gb[N_HEADS:2 * N_HEADS, :])
            g_scr[c] = li - _lane_cumsum(lf)
            lf_scr[c] = lf

    proj_scr[0:CARRY, 0:2 * QK] = carry_scr[h]
    wqk_scr[:, 0:QK] = wq_ref[...]
    wqk_scr[:, QK:2 * QK] = wk_ref[...]
    w_piece = {C_QK: wqk_scr, C_V: wv_ref, C_O: wo_ref, C_Z: wz_ref, C_U: wu_ref, C_VG: wvg_ref,
               C_ZG: wzg_ref}

    conv_w = conv_w_ref[...]
    conv_b = conv_b_ref[...]
    mnorm_w = mnorm_w_ref[...]
    sgu_w = sgu_w_ref[...]
    sgu_b = sgu_b_ref[...]
    w_sp = jnp.where(causal, w_sp_ref[...], 0.0).astype(BF16)
    b_sp = b_sp_ref[...]
    k_scale = QK ** -0.5

    state = {"c": c_scr[h],
             "n": n_scr[h][0:1, :],
             "m": m_scr[h][0:1, 0:1]}

    def rows_of(c):
        return pl.ds(CARRY + c * CHUNK, CHUNK)

    def tok_of(c):
        return pl.ds(c * CHUNK, CHUNK)

    def project(c, col):
        proj_scr[rows_of(c), col:col + PIECE_COLS] = jnp.dot(
            xn_scr[tok_of(c), :], w_piece[col][...], preferred_element_type=F32)

    def out_project_m(c):
        out_ref[tok_of(c), :] += jnp.dot(y_scr[tok_of(c), 0:DV], w_out_m_ref[...],
                                         preferred_element_type=F32)

    def out_project_g(c):
        out_ref[tok_of(c), :] += jnp.dot(y_scr[tok_of(c), DV:2 * DV], w_out_g_ref[...],
                                         preferred_element_type=F32)

    def out_project(c):
        y = y_scr[tok_of(c), :]
        out_ref[tok_of(c), :] += (
            jnp.dot(y[:, 0:DV], w_out_m_ref[...], preferred_element_type=F32)
            + jnp.dot(y[:, DV:2 * DV], w_out_g_ref[...], preferred_element_type=F32))

    def stage_a1(c, st):
        r0 = CARRY + c * CHUNK
        qk_pre = conv_b
        for j in range(CONV_W):
            qk_pre = qk_pre + conv_w[j:j + 1, :] * proj_scr[pl.ds(r0 - (CONV_W - 1) + j, CHUNK),
                                                             C_QK:C_QK + 2 * QK]
        qk_act = _silu(qk_pre)
        q = qk_act[:, 0:QK]
        k = qk_act[:, QK:2 * QK] * k_scale
        st["q_bf"] = q.astype(BF16)
        st["k"], st["k_bf"] = k, k.astype(BF16)

        g_row = g_scr[c, pl.ds(h, 1), :]
        lf_row = lf_scr[c, pl.ds(h, 1), :]
        m11 = state["m"]
        mx = jnp.maximum(jnp.max(jnp.where(causal, g_row, -jnp.inf), axis=1, keepdims=True), m11)
        b_col = jnp.sum(jnp.where(causal, lf_row, 0.0), axis=1, keepdims=True)
        st["p"] = jnp.exp(jnp.where(causal, g_row - mx, -jnp.inf))
        st["decay"] = jnp.exp(m11 - mx)
        st["floor"] = jnp.exp(-(b_col + mx))
        st["qn"] = jnp.sum(q * state["n"], axis=1, keepdims=True)
        state["m"] = b_col[CHUNK - 1:CHUNK, :] + mx[CHUNK - 1:CHUNK, :]

    def stage_a2(c, st):
        st["s_raw"] = lax.dot_general(st["q_bf"], st["k_bf"], (((1,), (1,)), ((), ())),
                                      preferred_element_type=F32)
        st["q_c"] = jnp.dot(st["q_bf"], state["c"].astype(BF16), preferred_element_type=F32)

    def stage_b1(c, st):
        st["v_bf"] = proj_scr[rows_of(c), C_V:C_V + DV].astype(BF16)
        p, decay = st["p"], st["decay"]
        s_qk = st["s_raw"] * p
        st["s_qk_bf"] = s_qk.astype(BF16)
        ws_row = p[CHUNK - 1:CHUNK, :]
        st["a11"] = decay[CHUNK - 1:CHUNK, :]
        st["kt_w"] = (st["k"].T * ws_row).astype(BF16)
        st["ws8"] = jnp.broadcast_to(ws_row, (8, CHUNK)).astype(BF16)
        st["den"] = jnp.sum(s_qk, axis=1, keepdims=True) + decay * st["qn"]

    def stage_b2(c, st):
        v_bf, a11 = st["v_bf"], st["a11"]
        st["num"] = (jnp.dot(st["s_qk_bf"], v_bf, preferred_element_type=F32)
                     + st["decay"] * st["q_c"])
        state["c"] = a11 * state["c"] + jnp.dot(st["kt_w"], v_bf, preferred_element_type=F32)
        state["n"] = a11 * state["n"] + jnp.dot(st["ws8"], st["k_bf"],
                                                preferred_element_type=F32)[0:1, :]

    def stage_c1(c, st):
        hm = st["num"] / jnp.maximum(jnp.abs(st["den"]), st["floor"])
        hm = hm * lax.rsqrt(jnp.mean(hm * hm, axis=-1, keepdims=True) + EPS) * mnorm_w
        y_m = (hm * _sigmoid(proj_scr[rows_of(c), C_O:C_O + DV])
               * _silu(proj_scr[rows_of(c), C_Z:C_Z + DV]))
        y_scr[tok_of(c), 0:DV] = y_m.astype(BF16)

    def stage_d1(c, st):
        gv = _gelu_tanh(proj_scr[rows_of(c), C_VG:C_VG + DV])
        mu = jnp.mean(gv, axis=-1, keepdims=True)
        gc = gv - mu
        var = jnp.mean(gc * gc, axis=-1, keepdims=True)
        st["vgn_bf"] = (gc * lax.rsqrt(var + EPS) * sgu_w + sgu_b).astype(BF16)

    def stage_d2(c, st):
        st["sv"] = jnp.dot(w_sp, st["vgn_bf"], preferred_element_type=F32) + b_sp

    def stage_e1(c, st):
        y_g = (_gelu_tanh(proj_scr[rows_of(c), C_U:C_U + DV]) * st["sv"]
               * _silu(proj_scr[rows_of(c), C_ZG:C_ZG + DV]))
        y_scr[tok_of(c), DV:2 * DV] = y_g.astype(BF16)

    piece_cols = (C_QK, C_V, C_O, C_Z, C_VG, C_U, C_ZG)
    order = ("w", "a2", "w", "b1", "w", "b2", "d1", "w", "c1", "d2", "w", "e1", "a1+", "w", "w")
    order_last = ("a2", "b1", "w", "b2", "d1", "w", "c1", "d2", "w", "e1")
    stage_fn = {"a2": stage_a2, "b1": stage_b1, "b2": stage_b2, "c1": stage_c1, "d1": stage_d1,
                "d2": stage_d2, "e1": stage_e1}
    for col in piece_cols:
        project(0, col)
    chunk_vals = [dict() for _ in range(n_chunks)]
    stage_a1(0, chunk_vals[0])
    for c in range(n_chunks):
        if c + 1 < n_chunks:
            mxu_work = [functools.partial(project, c + 1, col) for col in piece_cols]
        else:
            mxu_work = [functools.partial(out_project, cc) for cc in range(c)]
        for item in (order if c + 1 < n_chunks else order_last):
            if item == "w":
                if mxu_work:
                    mxu_work.pop(0)()
            elif item == "a1+":
                if c + 1 < n_chunks:
                    stage_a1(c + 1, chunk_vals[c + 1])
            else:
                stage_fn[item](c, chunk_vals[c])
        assert not mxu_work
    out_project(n_chunks - 1)

    c_scr[h] = state["c"]
    n_scr[h] = jnp.broadcast_to(state["n"], (8, QK))
    m_scr[h] = jnp.broadcast_to(state["m"], (8, QK))
    carry_scr[h] = proj_scr[tb:tb + CARRY, 0:2 * QK]

    @pl.when(h == N_HEADS - 1)
    def _block_epilogue():
        r = out_ref[...]
        ms = jnp.mean(r * r, axis=-1, keepdims=True)
        out_ref[...] = r * lax.rsqrt(ms + EPS) * fnorm_w_ref[...]


def kernel(x, norm_w, w_in, conv_w, conv_b, b_igate, b_fgate, mlstm_norm_w, sgu_norm_w,
           sgu_norm_b, w_spatial, b_spatial, w_out, final_norm_w):
    B, S, D = x.shape
    H = N_HEADS
    d_qk = H * QK
    d_m = H * DV
    assert D == d_m and w_in.shape == (D, 2 * d_qk + 3 * d_m + 2 * H + 3 * d_m)
    assert w_out.shape == (2 * d_m, D) and w_spatial.shape == (H, CHUNK, CHUNK)
    tb = SEQ_BLOCK
    assert S % tb == 0 and tb % CHUNK == 0

    o_q, o_k, o_v, o_o, o_z = 0, d_qk, 2 * d_qk, 2 * d_qk + d_m, 2 * d_qk + 2 * d_m
    o_i = 2 * d_qk + 3 * d_m
    o_u = o_i + 2 * H
    assert o_i % GATE_PAD == 0 and (o_k - o_q) % QK == 0
    assert all(o % DV == 0 for o in (o_v, o_o, o_z))

    w_bf = w_in.astype(BF16)
    w_b = w_bf[:, o_u:]
    w_g = w_bf[:, o_i:o_i + GATE_PAD]
    w_out_bf = w_out.astype(BF16)
    gate_b = jnp.concatenate([b_igate, b_fgate]).astype(F32).reshape(2 * H, 1)
    conv_w_h = jnp.concatenate([conv_w[:, :d_qk].reshape(CONV_W, H, QK),
                                conv_w[:, d_qk:].reshape(CONV_W, H, QK)],
                               axis=-1).transpose(1, 0, 2).astype(F32)
    conv_b_h = jnp.concatenate([conv_b[:d_qk].reshape(H, 1, QK),
                                conv_b[d_qk:].reshape(H, 1, QK)], axis=-1).astype(F32)

    grid = (B, S // tb, H)
    head3 = lambda b, s, h: (h, 0, 0)
    fixed2 = lambda b, s, h: (0, 0)
    tok3 = lambda b, s, h: (b, s, 0)

    return pl.pallas_call(
        functools.partial(_layer_kernel, tb=tb),
        out_shape=jax.ShapeDtypeStruct((B, S, D), x.dtype),
        grid=grid,
        in_specs=[
            pl.BlockSpec((None, tb, D), tok3),
            pl.BlockSpec((1, D), fixed2),
            pl.BlockSpec((D, QK), lambda b, s, h: (0, o_q // QK + h)),
            pl.BlockSpec((D, QK), lambda b, s, h: (0, o_k // QK + h)),
            pl.BlockSpec((D, DV), lambda b, s, h: (0, o_v // DV + h)),
            pl.BlockSpec((D, DV), lambda b, s, h: (0, o_o // DV + h)),
            pl.BlockSpec((D, DV), lambda b, s, h: (0, o_z // DV + h)),
            pl.BlockSpec((D, DV), lambda b, s, h: (0, h)),
            pl.BlockSpec((D, DV), lambda b, s, h: (0, H + h)),
            pl.BlockSpec((D, DV), lambda b, s, h: (0, 2 * H + h)),
            pl.BlockSpec((D, GATE_PAD), fixed2),
            pl.BlockSpec((2 * H, 1), fixed2),
            pl.BlockSpec((None, CONV_W, 2 * QK), head3),
            pl.BlockSpec((None, 1, 2 * QK), head3),
            pl.BlockSpec((None, 1, DV), head3),
            pl.BlockSpec((None, 1, DV), head3),
            pl.BlockSpec((None, 1, DV), head3),
            pl.BlockSpec((None, CHUNK, CHUNK), head3),
            pl.BlockSpec((None, CHUNK, 1), head3),
            pl.BlockSpec((DV, D), lambda b, s, h: (h, 0)),
            pl.BlockSpec((DV, D), lambda b, s, h: (H + h, 0)),
            pl.BlockSpec((1, D), fixed2),
        ],
        out_specs=pl.BlockSpec((None, tb, D), tok3),
        scratch_shapes=[
            pltpu.VMEM((tb, D), BF16),
            pltpu.VMEM((D, 2 * QK), BF16),
            pltpu.VMEM((CARRY + tb, HEAD_COLS), F32),
            pltpu.VMEM((tb // CHUNK, H, CHUNK), F32),
            pltpu.VMEM((tb // CHUNK, H, CHUNK), F32),
            pltpu.VMEM((H, QK, DV), F32),
            pltpu.VMEM((H, 8, QK), F32),
            pltpu.VMEM((H, 8, QK), F32),
            pltpu.VMEM((H, CARRY, 2 * QK), F32),
            pltpu.VMEM((tb, 2 * DV), BF16),
        ],
        compiler_params=pltpu.CompilerParams(
            dimension_semantics=("arbitrary", "arbitrary", "arbitrary"),
            vmem_limit_bytes=VMEM_LIMIT_BYTES),
        name="hybrid_layer",
    )(x, norm_w.reshape(1, D).astype(F32), w_bf, w_bf, w_bf, w_bf, w_bf, w_b, w_b, w_b, w_g,
      gate_b, conv_w_h, conv_b_h,
      mlstm_norm_w.reshape(H, 1, DV).astype(F32), sgu_norm_w.reshape(H, 1, DV).astype(F32),
      sgu_norm_b.reshape(H, 1, DV).astype(F32), w_spatial.astype(F32),
      b_spatial.reshape(H, CHUNK, 1).astype(F32), w_out_bf, w_out_bf,
      final_norm_w.reshape(1, D).astype(F32))
```

```python
import functools
import math

import jax
import jax.numpy as jnp
from jax import lax
from jax.experimental import pallas as pl
from jax.experimental.pallas import tpu as pltpu

F32 = jnp.float32
BF16 = jnp.bfloat16

N_HEADS = 8
QK = 128
DV = 256
CHUNK = 128
CONV_W = 4
EPS = 1e-6
GATE_PAD = 128
CARRY = 8

C_QK, C_V, C_O, C_Z, C_U, C_VG, C_ZG = 0, 256, 512, 768, 1024, 1280, 1536
HEAD_COLS = 1792
PIECE_COLS = 256

SEQ_BLOCK = 512
N_SEGMENTS = 7
N_SEGMENTS_FRONT = 4
GATE_ROWS = 16
PREP_ROWS = 128
VMEM_LIMIT_BYTES = 56 * 1024 * 1024


def _sigmoid(x):
    return 1.0 / (1.0 + jnp.exp(-x))


def _silu(x):
    return x * _sigmoid(x)


def _gelu_tanh(x):
    c = math.sqrt(2.0 / math.pi)
    return 0.5 * x * (1.0 + jnp.tanh(c * (x + 0.044715 * (x * x * x))))


def _log_sigmoid(x):
    return jnp.minimum(x, 0.0) - jnp.log1p(jnp.exp(-jnp.abs(x)))


def _lane_cumsum(x):
    lane = lax.broadcasted_iota(jnp.int32, x.shape, 1)
    k = 1
    while k < x.shape[-1]:
        x = x + jnp.where(lane >= k, pltpu.roll(x, k, 1), 0.0)
        k *= 2
    return x


def _weight_prep_kernel(a1_ref, a2_ref, b_ref, out_ref):
    seg = pl.program_id(0) % N_SEGMENTS

    @pl.when(seg < N_SEGMENTS_FRONT)
    def _aligned():
        rows = jnp.concatenate([a1_ref[...], a2_ref[...]], axis=0)
        out_ref[...] = rows.T.astype(BF16)

    @pl.when(seg >= N_SEGMENTS_FRONT)
    def _behind_gates():
        rows = jnp.concatenate([a1_ref[GATE_ROWS:, :], a2_ref[...], b_ref[...]], axis=0)
        out_ref[...] = rows.T.astype(BF16)


def _layer_kernel(x_ref, norms_ref, w_in_ref, w_g_ref, gate_b_ref, head_vec_ref, spatial_ref,
                  w_out_m_ref, w_out_g_ref,
                  out_ref,
                  xn_scr, proj_scr, g_scr, lf_scr, c_scr, n_scr, m_scr, carry_scr, y_scr,
                  *, tb):
    s_blk = pl.program_id(1)
    h = pl.program_id(2)
    n_chunks = tb // CHUNK

    row_id = lax.broadcasted_iota(jnp.int32, (CHUNK, CHUNK), 0)
    col_id = lax.broadcasted_iota(jnp.int32, (CHUNK, CHUNK), 1)
    causal = col_id <= row_id

    @pl.when(jnp.logical_and(s_blk == 0, h == 0))
    def _reset_state():
        c_scr[...] = jnp.zeros_like(c_scr)
        n_scr[...] = jnp.zeros_like(n_scr)
        m_scr[...] = jnp.zeros_like(m_scr)
        carry_scr[...] = jnp.zeros_like(carry_scr)

    @pl.when(h == 0)
    def _block_prologue():
        x = x_ref[...]
        ms = jnp.mean(x * x, axis=-1, keepdims=True)
        xn = x * lax.rsqrt(ms + EPS) * norms_ref[0:1, :]
        xn_bf = xn.astype(BF16)
        xn_scr[...] = xn_bf
        out_ref[...] = x
        gates = jnp.dot(xn_bf, w_g_ref[...], preferred_element_type=F32)
        gb = gate_b_ref[...]
        for c in range(n_chunks):
            gt = gates[c * CHUNK:(c + 1) * CHUNK, :].T
            li = gt[0:N_HEADS, :] + gb[0:N_HEADS, :]
            lf = _log_sigmoid(gt[N_HEADS:2 * N_HEADS, :] + gb[N_HEADS:2 * N_HEADS, :])
            g_scr[c] = li - _lane_cumsum(lf)
            lf_scr[c] = lf

    proj_scr[0:CARRY, 0:2 * QK] = carry_scr[h]
    conv_w = head_vec_ref[0:CONV_W, :]
    conv_b = head_vec_ref[CONV_W:CONV_W + 1, :]
    mnorm_w = head_vec_ref[CONV_W + 1:CONV_W + 2, :]
    sgu_w = head_vec_ref[CONV_W + 2:CONV_W + 3, :]
    sgu_b = head_vec_ref[CONV_W + 3:CONV_W + 4, :]
    w_sp = jnp.where(causal, spatial_ref[:, 0:CHUNK], 0.0).astype(BF16)
    b_sp = spatial_ref[:, CHUNK:CHUNK + 1]
    k_scale = QK ** -0.5

    state = {"c": c_scr[h],
             "n": n_scr[h][0:1, :],
             "m": m_scr[h][0:1, 0:1]}

    def rows_of(c):
        return pl.ds(CARRY + c * CHUNK, CHUNK)

    def tok_of(c):
        return pl.ds(c * CHUNK, CHUNK)

    def project(c, col):
        proj_scr[rows_of(c), col:col + PIECE_COLS] = jnp.dot(
            xn_scr[tok_of(c), :], w_in_ref[:, col:col + PIECE_COLS], preferred_element_type=F32)

    def out_project(c):
        y = y_scr[tok_of(c), :]
        out_ref[tok_of(c), :] += (
            jnp.dot(y[:, 0:DV], w_out_m_ref[...], preferred_element_type=F32)
            + jnp.dot(y[:, DV:2 * DV], w_out_g_ref[...], preferred_element_type=F32))

    def stage_a1(c, st):
        r0 = CARRY + c * CHUNK
        qk_pre = conv_b
        for j in range(CONV_W):
            qk_pre = qk_pre + conv_w[j:j + 1, :] * proj_scr[pl.ds(r0 - (CONV_W - 1) + j, CHUNK),
                                                             C_QK:C_QK + 2 * QK]
        qk_act = _silu(qk_pre)
        q = qk_act[:, 0:QK]
        k = qk_act[:, QK:2 * QK] * k_scale
        st["q_bf"] = q.astype(BF16)
        st["k"], st["k_bf"] = k, k.astype(BF16)

        g_row = g_scr[c, pl.ds(h, 1), :]
        lf_row = lf_scr[c, pl.ds(h, 1), :]
        m11 = state["m"]
        mx = jnp.maximum(jnp.max(jnp.where(causal, g_row, -jnp.inf), axis=1, keepdims=True), m11)
        b_col = jnp.sum(jnp.where(causal, lf_row, 0.0), axis=1, keepdims=True)
        st["p"] = jnp.exp(jnp.where(causal, g_row - mx, -jnp.inf))
        st["decay"] = jnp.exp(m11 - mx)
        st["floor"] = jnp.exp(-(b_col + mx))
        st["qn"] = jnp.sum(q * state["n"], axis=1, keepdims=True)
        state["m"] = b_col[CHUNK - 1:CHUNK, :] + mx[CHUNK - 1:CHUNK, :]

    def stage_a2(c, st):
        st["s_raw"] = lax.dot_general(st["q_bf"], st["k_bf"], (((1,), (1,)), ((), ())),
                                      preferred_element_type=F32)
        st["q_c"] = jnp.dot(st["q_bf"], state["c"].astype(BF16), preferred_element_type=F32)

    def stage_b1(c, st):
        st["v_bf"] = proj_scr[rows_of(c), C_V:C_V + DV].astype(BF16)
        p, decay = st["p"], st["decay"]
        s_qk = st["s_raw"] * p
        st["s_qk_bf"] = s_qk.astype(BF16)
        ws_row = p[CHUNK - 1:CHUNK, :]
        st["a11"] = decay[CHUNK - 1:CHUNK, :]
        st["kt_w"] = (st["k"].T * ws_row).astype(BF16)
        st["ws8"] = jnp.broadcast_to(ws_row, (8, CHUNK)).astype(BF16)
        st["den"] = jnp.sum(s_qk, axis=1, keepdims=True) + decay * st["qn"]

    def stage_b2(c, st):
        v_bf, a11 = st["v_bf"], st["a11"]
        st["num"] = (jnp.dot(st["s_qk_bf"], v_bf, preferred_element_type=F32)
                     + st["decay"] * st["q_c"])
        state["c"] = a11 * state["c"] + jnp.dot(st["kt_w"], v_bf, preferred_element_type=F32)
        state["n"] = a11 * state["n"] + jnp.dot(st["ws8"], st["k_bf"],
                                                preferred_element_type=F32)[0:1, :]

    def stage_c1(c, st):
        hm = st["num"] / jnp.maximum(jnp.abs(st["den"]), st["floor"])
        hm = hm * lax.rsqrt(jnp.mean(hm * hm, axis=-1, keepdims=True) + EPS) * mnorm_w
        y_m = (hm * _sigmoid(proj_scr[rows_of(c), C_O:C_O + DV])
               * _silu(proj_scr[rows_of(c), C_Z:C_Z + DV]))
        y_scr[tok_of(c), 0:DV] = y_m.astype(BF16)

    def stage_d1(c, st):
        gv = _gelu_tanh(proj_scr[rows_of(c), C_VG:C_VG + DV])
        mu = jnp.mean(gv, axis=-1, keepdims=True)
        gc = gv - mu
        var = jnp.mean(gc * gc, axis=-1, keepdims=True)
        st["vgn_bf"] = (gc * lax.rsqrt(var + EPS) * sgu_w + sgu_b).astype(BF16)

    def stage_d2(c, st):
        st["sv"] = jnp.dot(w_sp, st["vgn_bf"], preferred_element_type=F32) + b_sp

    def stage_e1(c, st):
        y_g = (_gelu_tanh(proj_scr[rows_of(c), C_U:C_U + DV]) * st["sv"]
               * _silu(proj_scr[rows_of(c), C_ZG:C_ZG + DV]))
        y_scr[tok_of(c), DV:2 * DV] = y_g.astype(BF16)

    piece_cols = (C_QK, C_V, C_O, C_Z, C_VG, C_U, C_ZG)
    order = ("w", "a2", "w", "b1", "w", "b2", "d1", "w", "c1", "d2", "w", "e1", "a1+", "w", "w")
    order_last = ("a2", "b1", "w", "b2", "d1", "w", "c1", "d2", "w", "e1")
    stage_fn = {"a2": stage_a2, "b1": stage_b1, "b2": stage_b2, "c1": stage_c1, "d1": stage_d1,
                "d2": stage_d2, "e1": stage_e1}
    for col in piece_cols:
        project(0, col)
    chunk_vals = [dict() for _ in range(n_chunks)]
    stage_a1(0, chunk_vals[0])
    for c in range(n_chunks):
        if c + 1 < n_chunks:
            mxu_work = [functools.partial(project, c + 1, col) for col in piece_cols]
        else:
            mxu_work = [functools.partial(out_project, cc) for cc in range(c)]
        for item in (order if c + 1 < n_chunks else order_last):
            if item == "w":
                if mxu_work:
                    mxu_work.pop(0)()
            elif item == "a1+":
                if c + 1 < n_chunks:
                    stage_a1(c + 1, chunk_vals[c + 1])
            else:
                stage_fn[item](c, chunk_vals[c])
        assert not mxu_work
    out_project(n_chunks - 1)

    c_scr[h] = state["c"]
    n_scr[h] = jnp.broadcast_to(state["n"], (8, QK))
    m_scr[h] = jnp.broadcast_to(state["m"], (8, QK))
    carry_scr[h] = proj_scr[tb:tb + CARRY, 0:2 * QK]

    @pl.when(h == N_HEADS - 1)
    def _block_epilogue():
        r = out_ref[...]
        ms = jnp.mean(r * r, axis=-1, keepdims=True)
        out_ref[...] = r * lax.rsqrt(ms + EPS) * norms_ref[1:2, :]


def kernel(x, norm_w, w_in, conv_w, conv_b, b_igate, b_fgate, mlstm_norm_w, sgu_norm_w,
           sgu_norm_b, w_spatial, b_spatial, w_out, final_norm_w):
    B, S, D = x.shape
    H = N_HEADS
    d_qk = H * QK
    d_m = H * DV
    assert D == d_m and w_in.shape == (D, 2 * d_qk + 3 * d_m + 2 * H + 3 * d_m)
    assert w_out.shape == (2 * d_m, D) and w_spatial.shape == (H, CHUNK, CHUNK)
    tb = SEQ_BLOCK
    assert S % tb == 0 and tb % CHUNK == 0

    o_q, o_k, o_v, o_o, o_z = 0, d_qk, 2 * d_qk, 2 * d_qk + d_m, 2 * d_qk + 2 * d_m
    o_i = 2 * d_qk + 3 * d_m
    o_u = o_i + 2 * H
    assert o_i % GATE_PAD == 0 and (o_k - o_q) % QK == 0
    assert all(o % DV == 0 for o in (o_v, o_o, o_z))

    assert o_u - o_i == GATE_ROWS and o_i % PREP_ROWS == 0 and PREP_ROWS % GATE_ROWS == 0
    n_head_blocks = H * N_SEGMENTS
    blk = lambda off: off // PREP_ROWS
    per_seg_a1 = (blk(o_q), blk(o_v), blk(o_o), blk(o_z), blk(o_i), blk(o_i + d_m),
                  blk(o_i + 2 * d_m))
    per_seg_a2 = (blk(o_k),) + tuple(a + 1 for a in per_seg_a1[1:])
    per_seg_step = (1, 2, 2, 2, 2, 2, 2)

    def _pick(values, seg):
        out = jnp.int32(values[0])
        for n, v in enumerate(values[1:], start=1):
            out = jnp.where(seg == n, jnp.int32(v), out)
        return out

    def _a_index(per_seg):
        def index_map(t):
            head, seg = t // N_SEGMENTS, t % N_SEGMENTS
            in_head = _pick(per_seg, seg) + _pick(per_seg_step, seg) * head
            gate_blk = blk(o_i) + (0 if per_seg is per_seg_a1 else 1)
            return jnp.where(t == n_head_blocks, gate_blk, in_head), 0
        return index_map

    def _b_index(t):
        a2 = _a_index(per_seg_a2)(jnp.minimum(t, n_head_blocks - 1))[0]
        last = w_in.shape[1] // GATE_ROWS - 1
        return jnp.minimum((a2 + 1) * (PREP_ROWS // GATE_ROWS), last), 0

    w_t = w_in.T
    w_all = pl.pallas_call(
        _weight_prep_kernel,
        out_shape=jax.ShapeDtypeStruct((D, (n_head_blocks + 1) * PIECE_COLS), BF16),
        grid=(n_head_blocks + 1,),
        in_specs=[pl.BlockSpec((PREP_ROWS, D), _a_index(per_seg_a1)),
                  pl.BlockSpec((PREP_ROWS, D), _a_index(per_seg_a2)),
                  pl.BlockSpec((GATE_ROWS, D), _b_index)],
        out_specs=pl.BlockSpec((D, PIECE_COLS), lambda t: (0, t)),
        compiler_params=pltpu.CompilerParams(dimension_semantics=("arbitrary",)),
        name="weight_prep",
    )(w_t, w_t, w_t)

    w_out_bf = w_out.astype(BF16)
    gate_b = jnp.concatenate([b_igate, b_fgate]).astype(F32).reshape(2 * H, 1)
    norms = jnp.stack([norm_w, final_norm_w]).astype(F32)
    conv_qk = jnp.concatenate([conv_w[:, :d_qk].reshape(CONV_W, H, QK),
                               conv_w[:, d_qk:].reshape(CONV_W, H, QK)], axis=-1)
    conv_b_qk = jnp.concatenate([conv_b[:d_qk].reshape(1, H, QK),
                                 conv_b[d_qk:].reshape(1, H, QK)], axis=-1)
    head_vec = jnp.concatenate(
        [conv_qk, conv_b_qk, mlstm_norm_w.reshape(1, H, DV), sgu_norm_w.reshape(1, H, DV),
         sgu_norm_b.reshape(1, H, DV)], axis=0).transpose(1, 0, 2).astype(F32)
    spatial = jnp.concatenate(
        [w_spatial, b_spatial.reshape(H, CHUNK, 1),
         jnp.zeros((H, CHUNK, CHUNK - 1), w_spatial.dtype)], axis=-1).astype(F32)

    grid = (B, S // tb, H)
    head3 = lambda b, s, h: (h, 0, 0)
    fixed2 = lambda b, s, h: (0, 0)
    tok3 = lambda b, s, h: (b, s, 0)

    return pl.pallas_call(
        functools.partial(_layer_kernel, tb=tb),
        out_shape=jax.ShapeDtypeStruct((B, S, D), x.dtype),
        grid=grid,
        in_specs=[
            pl.BlockSpec((None, tb, D), tok3),
            pl.BlockSpec((2, D), fixed2),
            pl.BlockSpec((D, HEAD_COLS), lambda b, s, h: (0, h)),
            pl.BlockSpec((D, GATE_PAD), lambda b, s, h: (0, H * HEAD_COLS // GATE_PAD)),
            pl.BlockSpec((2 * H, 1), fixed2),
            pl.BlockSpec((None, CONV_W + 4, 2 * QK), head3),
            pl.BlockSpec((None, CHUNK, 2 * CHUNK), head3),
            pl.BlockSpec((DV, D), lambda b, s, h: (h, 0)),
            pl.BlockSpec((DV, D), lambda b, s, h: (H + h, 0)),
        ],
        out_specs=pl.BlockSpec((None, tb, D), tok3),
        scratch_shapes=[
            pltpu.VMEM((tb, D), BF16),
            pltpu.VMEM((CARRY + tb, HEAD_COLS), F32),
            pltpu.VMEM((tb // CHUNK, H, CHUNK), F32),
            pltpu.VMEM((tb // CHUNK, H, CHUNK), F32),
            pltpu.VMEM((H, QK, DV), F32),
            pltpu.VMEM((H, 8, QK), F32),
            pltpu.VMEM((H, 8, QK), F32),
            pltpu.VMEM((H, CARRY, 2 * QK), F32),
            pltpu.VMEM((tb, 2 * DV), BF16),
        ],
        compiler_params=pltpu.CompilerParams(
            dimension_semantics=("arbitrary", "arbitrary", "arbitrary"),
            vmem_limit_bytes=VMEM_LIMIT_BYTES),
        name="hybrid_layer",
    )(x, norms, w_all, w_all, gate_b, head_vec, spatial, w_out_bf, w_out_bf)
```

```python
import functools
import math

import jax
import jax.numpy as jnp
from jax import lax
from jax.experimental import pallas as pl
from jax.experimental.pallas import tpu as pltpu

F32 = jnp.float32
BF16 = jnp.bfloat16

N_HEADS = 8
QK = 128
DV = 256
CHUNK = 128
CONV_W = 4
EPS = 1e-6
GATE_PAD = 128
CARRY = 8

C_QK, C_V, C_O, C_Z, C_U, C_VG, C_ZG = 0, 256, 512, 768, 1024, 1280, 1536
HEAD_COLS = 1792
PIECE_COLS = 256

SEQ_BLOCK = 1024
N_SEGMENTS = 7
N_SEGMENTS_FRONT = 4
GATE_ROWS = 16
PREP_ROWS = 128
VMEM_LIMIT_BYTES = 56 * 1024 * 1024


def _sigmoid(x):
    return 1.0 / (1.0 + jnp.exp(-x))


def _silu(x):
    return x * _sigmoid(x)


def _gelu_tanh(x):
    c = math.sqrt(2.0 / math.pi)
    return 0.5 * x * (1.0 + jnp.tanh(c * (x + 0.044715 * (x * x * x))))


def _log_sigmoid(x):
    return jnp.minimum(x, 0.0) - jnp.log1p(jnp.exp(-jnp.abs(x)))


def _lane_cumsum(x):
    lane = lax.broadcasted_iota(jnp.int32, x.shape, 1)
    k = 1
    while k < x.shape[-1]:
        x = x + jnp.where(lane >= k, pltpu.roll(x, k, 1), 0.0)
        k *= 2
    return x


def _weight_prep_kernel(a1_ref, a2_ref, b_ref, out_ref):
    seg = pl.program_id(0) % N_SEGMENTS

    @pl.when(seg < N_SEGMENTS_FRONT)
    def _aligned():
        rows = jnp.concatenate([a1_ref[...], a2_ref[...]], axis=0)
        out_ref[...] = rows.T.astype(BF16)

    @pl.when(seg >= N_SEGMENTS_FRONT)
    def _behind_gates():
        rows = jnp.concatenate([a1_ref[GATE_ROWS:, :], a2_ref[...], b_ref[...]], axis=0)
        out_ref[...] = rows.T.astype(BF16)


def _layer_kernel(x_hbm, norms_ref, w_in_ref, w_g_ref, gate_b_ref, head_vec_ref, spatial_ref,
                  w_out_m_ref, w_out_g_ref,
                  out_ref,
                  x_buf, x_sem, xn_scr, proj_scr, g_scr, lf_scr, c_scr, n_scr, m_scr, carry_scr, y_scr,
                  *, tb):
    b_idx = pl.program_id(0)
    s_blk = pl.program_id(1)
    h = pl.program_id(2)
    n_chunks = tb // CHUNK
    n_b, n_s = pl.num_programs(0), pl.num_programs(1)

    def x_copy(bb, ss):
        return pltpu.make_async_copy(x_hbm.at[bb, pl.ds(pl.multiple_of(ss * tb, tb), tb), :],
                                     x_buf, x_sem.at[0])

    row_id = lax.broadcasted_iota(jnp.int32, (CHUNK, CHUNK), 0)
    col_id = lax.broadcasted_iota(jnp.int32, (CHUNK, CHUNK), 1)
    causal = col_id <= row_id

    @pl.when(jnp.logical_and(s_blk == 0, h == 0))
    def _reset_state():
        c_scr[...] = jnp.zeros_like(c_scr)
        n_scr[...] = jnp.zeros_like(n_scr)
        m_scr[...] = jnp.zeros_like(m_scr)
        carry_scr[...] = jnp.zeros_like(carry_scr)

    @pl.when(jnp.logical_and(h == 0, jnp.logical_and(b_idx == 0, s_blk == 0)))
    def _fetch_first_block():
        x_copy(0, 0).start()

    @pl.when(jnp.logical_and(h == 1, jnp.logical_or(b_idx < n_b - 1, s_blk < n_s - 1)))
    def _prefetch_next_block():
        wrap = s_blk == n_s - 1
        x_copy(jnp.where(wrap, b_idx + 1, b_idx), jnp.where(wrap, 0, s_blk + 1)).start()

    @pl.when(h == 0)
    def _block_prologue():
        x_copy(b_idx, s_blk).wait()
        x = x_buf[...]
        ms = jnp.mean(x * x, axis=-1, keepdims=True)
        xn = x * lax.rsqrt(ms + EPS) * norms_ref[0:1, :]
        xn_bf = xn.astype(BF16)
        xn_scr[...] = xn_bf
        out_ref[...] = x
        gates = jnp.dot(xn_bf, w_g_ref[...], preferred_element_type=F32)
        gb = gate_b_ref[...]
        for c in range(n_chunks):
            gt = gates[c * CHUNK:(c + 1) * CHUNK, :].T
            li = gt[0:N_HEADS, :] + gb[0:N_HEADS, :]
            lf = _log_sigmoid(gt[N_HEADS:2 * N_HEADS, :] + gb[N_HEADS:2 * N_HEADS, :])
            g_scr[c] = li - _lane_cumsum(lf)
            lf_scr[c] = lf

    proj_scr[0:CARRY, 0:2 * QK] = carry_scr[h]
    conv_w = head_vec_ref[0:CONV_W, :]
    conv_b = head_vec_ref[CONV_W:CONV_W + 1, :]
    mnorm_w = head_vec_ref[CONV_W + 1:CONV_W + 2, :]
    sgu_w = head_vec_ref[CONV_W + 2:CONV_W + 3, :]
    sgu_b = head_vec_ref[CONV_W + 3:CONV_W + 4, :]
    w_sp = jnp.where(causal, spatial_ref[:, 0:CHUNK], 0.0).astype(BF16)
    b_sp = spatial_ref[:, CHUNK:CHUNK + 1]
    k_scale = QK ** -0.5

    state = {"c": c_scr[h],
             "n": n_scr[h][0:1, :],
             "m": m_scr[h][0:1, 0:1]}

    def rows_of(c):
        return pl.ds(CARRY + (c % 2) * CHUNK, CHUNK)

    def tok_of(c):
        return pl.ds(c * CHUNK, CHUNK)

    def project(c, col):
        proj_scr[rows_of(c), col:col + PIECE_COLS] = jnp.dot(
            xn_scr[tok_of(c), :], w_in_ref[:, col:col + PIECE_COLS], preferred_element_type=F32)

    def out_project(c):
        y = y_scr[tok_of(c), :]
        out_ref[tok_of(c), :] += (
            jnp.dot(y[:, 0:DV], w_out_m_ref[...], preferred_element_type=F32)
            + jnp.dot(y[:, DV:2 * DV], w_out_g_ref[...], preferred_element_type=F32))

    def stage_a1(c, st):
        r0 = CARRY + (c % 2) * CHUNK
        if c >= 2 and c % 2 == 0:
            proj_scr[0:CARRY, 0:2 * QK] = proj_scr[2 * CHUNK:2 * CHUNK + CARRY, 0:2 * QK]
        qk_pre = conv_b
        for j in range(CONV_W):
            qk_pre = qk_pre + conv_w[j:j + 1, :] * proj_scr[pl.ds(r0 - (CONV_W - 1) + j, CHUNK),
                                                             C_QK:C_QK + 2 * QK]
        qk_act = _silu(qk_pre)
        q = qk_act[:, 0:QK]
        k = qk_act[:, QK:2 * QK] * k_scale
        st["q_bf"] = q.astype(BF16)
        st["k"], st["k_bf"] = k, k.astype(BF16)

        g_row = g_scr[c, pl.ds(h, 1), :]
        lf_row = lf_scr[c, pl.ds(h, 1), :]
        m11 = state["m"]
        mx = jnp.maximum(jnp.max(jnp.where(causal, g_row, -jnp.inf), axis=1, keepdims=True), m11)
        b_col = jnp.sum(jnp.where(causal, lf_row, 0.0), axis=1, keepdims=True)
        st["p"] = jnp.exp(jnp.where(causal, g_row - mx, -jnp.inf))
        st["decay"] = jnp.exp(m11 - mx)
        st["floor"] = jnp.exp(-(b_col + mx))
        st["qn"] = jnp.sum(q * state["n"], axis=1, keepdims=True)
        state["m"] = b_col[CHUNK - 1:CHUNK, :] + mx[CHUNK - 1:CHUNK, :]

    def stage_a2(c, st):
        st["s_raw"] = lax.dot_general(st["q_bf"], st["k_bf"], (((1,), (1,)), ((), ())),
                                      preferred_element_type=F32)
        st["q_c"] = jnp.dot(st["q_bf"], state["c"].astype(BF16), preferred_element_type=F32)

    def stage_b1(c, st):
        st["v_bf"] = proj_scr[rows_of(c), C_V:C_V + DV].astype(BF16)
        p, decay = st["p"], st["decay"]
        s_qk = st["s_raw"] * p
        st["s_qk_bf"] = s_qk.astype(BF16)
        ws_row = p[CHUNK - 1:CHUNK, :]
        st["a11"] = decay[CHUNK - 1:CHUNK, :]
        st["kt_w"] = (st["k"].T * ws_row).astype(BF16)
        st["ws8"] = jnp.broadcast_to(ws_row, (8, CHUNK)).astype(BF16)
        st["den"] = jnp.sum(s_qk, axis=1, keepdims=True) + decay * st["qn"]

    def stage_b2(c, st):
        v_bf, a11 = st["v_bf"], st["a11"]
        st["num"] = (jnp.dot(st["s_qk_bf"], v_bf, preferred_element_type=F32)
                     + st["decay"] * st["q_c"])
        state["c"] = a11 * state["c"] + jnp.dot(st["kt_w"], v_bf, preferred_element_type=F32)
        state["n"] = a11 * state["n"] + jnp.dot(st["ws8"], st["k_bf"],
                                                preferred_element_type=F32)[0:1, :]

    def stage_c1(c, st):
        hm = st["num"] / jnp.maximum(jnp.abs(st["den"]), st["floor"])
        hm = hm * lax.rsqrt(jnp.mean(hm * hm, axis=-1, keepdims=True) + EPS) * mnorm_w
        y_m = (hm * _sigmoid(proj_scr[rows_of(c), C_O:C_O + DV])
               * _silu(proj_scr[rows_of(c), C_Z:C_Z + DV]))
        y_scr[tok_of(c), 0:DV] = y_m.astype(BF16)

    def stage_d1(c, st):
        gv = _gelu_tanh(proj_scr[rows_of(c), C_VG:C_VG + DV])
        mu = jnp.mean(gv, axis=-1, keepdims=True)
        gc = gv - mu
        var = jnp.mean(gc * gc, axis=-1, keepdims=True)
        st["vgn_bf"] = (gc * lax.rsqrt(var + EPS) * sgu_w + sgu_b).astype(BF16)

    def stage_d2(c, st):
        st["sv"] = jnp.dot(w_sp, st["vgn_bf"], preferred_element_type=F32) + b_sp

    def stage_e1(c, st):
        y_g = (_gelu_tanh(proj_scr[rows_of(c), C_U:C_U + DV]) * st["sv"]
               * _silu(proj_scr[rows_of(c), C_ZG:C_ZG + DV]))
        y_scr[tok_of(c), DV:2 * DV] = y_g.astype(BF16)

    piece_cols = (C_QK, C_V, C_O, C_Z, C_VG, C_U, C_ZG)
    order = ("w", "a2", "w", "b1", "w", "b2", "d1", "w", "c1", "d2", "w", "e1", "a1+", "w", "w")
    order_last = ("a2", "b1", "w", "b2", "d1", "w", "c1", "d2", "w", "e1")
    stage_fn = {"a2": stage_a2, "b1": stage_b1, "b2": stage_b2, "c1": stage_c1, "d1": stage_d1,
                "d2": stage_d2, "e1": stage_e1}
    for col in piece_cols:
        project(0, col)
    chunk_vals = [dict() for _ in range(n_chunks)]
    stage_a1(0, chunk_vals[0])
    for c in range(n_chunks):
        if c + 1 < n_chunks:
            mxu_work = [functools.partial(project, c + 1, col) for col in piece_cols]
            if 1 <= c <= n_chunks - 4:
                mxu_work.append(functools.partial(out_project, c - 1))
        else:
            mxu_work = [functools.partial(out_project, cc) for cc in range(max(c - 3, 0), c)]
        for item in (order if c + 1 < n_chunks else order_last):
            if item == "w":
                if mxu_work:
                    mxu_work.pop(0)()
            elif item == "a1+":
                if c + 1 < n_chunks:
                    stage_a1(c + 1, chunk_vals[c + 1])
            else:
                stage_fn[item](c, chunk_vals[c])
        for work in mxu_work:
            work()
    out_project(n_chunks - 1)

    c_scr[h] = state["c"]
    n_scr[h] = jnp.broadcast_to(state["n"], (8, QK))
    m_scr[h] = jnp.broadcast_to(state["m"], (8, QK))
    last_end = CARRY + ((n_chunks - 1) % 2 + 1) * CHUNK
    carry_scr[h] = proj_scr[last_end - CARRY:last_end, 0:2 * QK]

    @pl.when(h == N_HEADS - 1)
    def _block_epilogue():
        r = out_ref[...]
        ms = jnp.mean(r * r, axis=-1, keepdims=True)
        out_ref[...] = r * lax.rsqrt(ms + EPS) * norms_ref[1:2, :]


def kernel(x, norm_w, w_in, conv_w, conv_b, b_igate, b_fgate, mlstm_norm_w, sgu_norm_w,
           sgu_norm_b, w_spatial, b_spatial, w_out, final_norm_w):
    B, S, D = x.shape
    H = N_HEADS
    d_qk = H * QK
    d_m = H * DV
    assert D == d_m and w_in.shape == (D, 2 * d_qk + 3 * d_m + 2 * H + 3 * d_m)
    assert w_out.shape == (2 * d_m, D) and w_spatial.shape == (H, CHUNK, CHUNK)
    tb = SEQ_BLOCK
    assert S % tb == 0 and tb % CHUNK == 0

    o_q, o_k, o_v, o_o, o_z = 0, d_qk, 2 * d_qk, 2 * d_qk + d_m, 2 * d_qk + 2 * d_m
    o_i = 2 * d_qk + 3 * d_m
    o_u = o_i + 2 * H
    assert o_i % GATE_PAD == 0 and (o_k - o_q) % QK == 0
    assert all(o % DV == 0 for o in (o_v, o_o, o_z))

    assert o_u - o_i == GATE_ROWS and o_i % PREP_ROWS == 0 and PREP_ROWS % GATE_ROWS == 0
    n_head_blocks = H * N_SEGMENTS
    blk = lambda off: off // PREP_ROWS
    per_seg_a1 = (blk(o_q), blk(o_v), blk(o_o), blk(o_z), blk(o_i), blk(o_i + d_m),
                  blk(o_i + 2 * d_m))
    per_seg_a2 = (blk(o_k),) + tuple(a + 1 for a in per_seg_a1[1:])
    per_seg_step = (1, 2, 2, 2, 2, 2, 2)

    def _pick(values, seg):
        out = jnp.int32(values[0])
        for n, v in enumerate(values[1:], start=1):
            out = jnp.where(seg == n, jnp.int32(v), out)
        return out

    def _a_index(per_seg):
        def index_map(t):
            head, seg = t // N_SEGMENTS, t % N_SEGMENTS
            in_head = _pick(per_seg, seg) + _pick(per_seg_step, seg) * head
            gate_blk = blk(o_i) + (0 if per_seg is per_seg_a1 else 1)
            return jnp.where(t == n_head_blocks, gate_blk, in_head), 0
        return index_map

    def _b_index(t):
        a2 = _a_index(per_seg_a2)(jnp.minimum(t, n_head_blocks - 1))[0]
        last = w_in.shape[1] // GATE_ROWS - 1
        return jnp.minimum((a2 + 1) * (PREP_ROWS // GATE_ROWS), last), 0

    w_t = w_in.T
    w_all = pl.pallas_call(
        _weight_prep_kernel,
        out_shape=jax.ShapeDtypeStruct((D, (n_head_blocks + 1) * PIECE_COLS), BF16),
        grid=(n_head_blocks + 1,),
        in_specs=[pl.BlockSpec((PREP_ROWS, D), _a_index(per_seg_a1)),
                  pl.BlockSpec((PREP_ROWS, D), _a_index(per_seg_a2)),
                  pl.BlockSpec((GATE_ROWS, D), _b_index)],
        out_specs=pl.BlockSpec((D, PIECE_COLS), lambda t: (0, t)),
        compiler_params=pltpu.CompilerParams(dimension_semantics=("arbitrary",)),
        name="weight_prep",
    )(w_t, w_t, w_t)

    w_out_bf = w_out.astype(BF16)
    gate_b = jnp.concatenate([b_igate, b_fgate]).astype(F32).reshape(2 * H, 1)
    norms = jnp.stack([norm_w, final_norm_w]).astype(F32)
    conv_qk = jnp.concatenate([conv_w[:, :d_qk].reshape(CONV_W, H, QK),
                               conv_w[:, d_qk:].reshape(CONV_W, H, QK)], axis=-1)
    conv_b_qk = jnp.concatenate([conv_b[:d_qk].reshape(1, H, QK),
                                 conv_b[d_qk:].reshape(1, H, QK)], axis=-1)
    head_vec = jnp.concatenate(
        [conv_qk, conv_b_qk, mlstm_norm_w.reshape(1, H, DV), sgu_norm_w.reshape(1, H, DV),
         sgu_norm_b.reshape(1, H, DV)], axis=0).transpose(1, 0, 2).astype(F32)
    spatial = jnp.concatenate(
        [w_spatial, b_spatial.reshape(H, CHUNK, 1),
         jnp.zeros((H, CHUNK, CHUNK - 1), w_spatial.dtype)], axis=-1).astype(F32)

    grid = (B, S // tb, H)
    head3 = lambda b, s, h: (h, 0, 0)
    fixed2 = lambda b, s, h: (0, 0)
    tok3 = lambda b, s, h: (b, s, 0)

    return pl.pallas_call(
        functools.partial(_layer_kernel, tb=tb),
        out_shape=jax.ShapeDtypeStruct((B, S, D), x.dtype),
        grid=grid,
        in_specs=[
            pl.BlockSpec(memory_space=pl.ANY),
            pl.BlockSpec((2, D), fixed2),
            pl.BlockSpec((D, HEAD_COLS), lambda b, s, h: (0, h)),
            pl.BlockSpec((D, GATE_PAD), lambda b, s, h: (0, H * HEAD_COLS // GATE_PAD)),
            pl.BlockSpec((2 * H, 1), fixed2),
            pl.BlockSpec((None, CONV_W + 4, 2 * QK), head3),
            pl.BlockSpec((None, CHUNK, 2 * CHUNK), head3),
            pl.BlockSpec((DV, D), lambda b, s, h: (h, 0)),
            pl.BlockSpec((DV, D), lambda b, s, h: (H + h, 0)),
        ],
        out_specs=pl.BlockSpec((None, tb, D), tok3),
        scratch_shapes=[
            pltpu.VMEM((tb, D), x.dtype),
            pltpu.SemaphoreType.DMA((1,)),
            pltpu.VMEM((tb, D), BF16),
            pltpu.VMEM((CARRY + 2 * CHUNK, HEAD_COLS), F32),
            pltpu.VMEM((tb // CHUNK, H, CHUNK), F32),
            pltpu.VMEM((tb // CHUNK, H, CHUNK), F32),
            pltpu.VMEM((H, QK, DV), F32),
            pltpu.VMEM((H, 8, QK), F32),
            pltpu.VMEM((H, 8, QK), F32),
            pltpu.VMEM((H, CARRY, 2 * QK), F32),
            pltpu.VMEM((tb, 2 * DV), BF16),
        ],
        compiler_params=pltpu.CompilerParams(
            dimension_semantics=("arbitrary", "arbitrary", "arbitrary"),
            vmem_limit_bytes=VMEM_LIMIT_BYTES),
        name="hybrid_layer",
    )(x, norms, w_all, w_all, gate_b, head_vec, spatial, w_out_bf, w_out_bf)
```

```python
import functools
import math

import jax
import jax.numpy as jnp
from jax import lax
from jax.experimental import pallas as pl
from jax.experimental.pallas import tpu as pltpu

F32 = jnp.float32
BF16 = jnp.bfloat16

N_HEADS = 8
QK = 128
DV = 256
CHUNK = 128
CONV_W = 4
EPS = 1e-6
GATE_PAD = 128
CARRY = 8

C_QK, C_V, C_O, C_Z, C_U, C_VG, C_ZG = 0, 256, 512, 768, 1024, 1280, 1536
HEAD_COLS = 1792
PIECE_COLS = 256

SEQ_BLOCK = 1024
N_SEGMENTS = 7
N_SEGMENTS_FRONT = 4
GATE_ROWS = 16
PREP_ROWS = 128
VMEM_LIMIT_BYTES = 56 * 1024 * 1024


def _sigmoid(x):
    return 1.0 / (1.0 + jnp.exp(-x))


def _silu(x):
    return x * _sigmoid(x)


def _gelu_tanh(x):
    c = math.sqrt(2.0 / math.pi)
    return 0.5 * x * (1.0 + jnp.tanh(c * (x + 0.044715 * (x * x * x))))


def _log_sigmoid(x):
    return jnp.minimum(x, 0.0) - jnp.log1p(jnp.exp(-jnp.abs(x)))


def _lane_cumsum(x):
    lane = lax.broadcasted_iota(jnp.int32, x.shape, 1)
    k = 1
    while k < x.shape[-1]:
        x = x + jnp.where(lane >= k, pltpu.roll(x, k, 1), 0.0)
        k *= 2
    return x


def _weight_prep_kernel(a1_ref, a2_ref, b_ref, out_ref):
    seg = pl.program_id(0) % N_SEGMENTS

    @pl.when(seg < N_SEGMENTS_FRONT)
    def _aligned():
        rows = jnp.concatenate([a1_ref[...], a2_ref[...]], axis=0)
        out_ref[...] = rows.T.astype(BF16)

    @pl.when(seg >= N_SEGMENTS_FRONT)
    def _behind_gates():
        rows = jnp.concatenate([a1_ref[GATE_ROWS:, :], a2_ref[...], b_ref[...]], axis=0)
        out_ref[...] = rows.T.astype(BF16)


def _layer_kernel(x_hbm, norms_ref, w_in_ref, w_g_ref, gate_b_ref, head_vec_ref, spatial_ref,
                  w_out_m_ref, w_out_g_ref,
                  out_ref,
                  x_buf, x_sem, xn_scr, proj_scr, g_scr, lf_scr, c_scr, n_scr, m_scr, carry_scr, y_scr,
                  *, tb):
    b_idx = pl.program_id(0)
    s_blk = pl.program_id(1)
    h = pl.program_id(2)
    n_chunks = tb // CHUNK
    n_b, n_s = pl.num_programs(0), pl.num_programs(1)

    def x_copy(bb, ss):
        return pltpu.make_async_copy(x_hbm.at[bb, pl.ds(pl.multiple_of(ss * tb, tb), tb), :],
                                     x_buf, x_sem.at[0])

    row_id = lax.broadcasted_iota(jnp.int32, (CHUNK, CHUNK), 0)
    col_id = lax.broadcasted_iota(jnp.int32, (CHUNK, CHUNK), 1)
    causal = col_id <= row_id

    @pl.when(jnp.logical_and(s_blk == 0, h == 0))
    def _reset_state():
        c_scr[...] = jnp.zeros_like(c_scr)
        n_scr[...] = jnp.zeros_like(n_scr)
        m_scr[...] = jnp.zeros_like(m_scr)
        carry_scr[...] = jnp.zeros_like(carry_scr)

    @pl.when(jnp.logical_and(h == 0, jnp.logical_and(b_idx == 0, s_blk == 0)))
    def _fetch_first_block():
        x_copy(0, 0).start()

    @pl.when(jnp.logical_and(h == 1, jnp.logical_or(b_idx < n_b - 1, s_blk < n_s - 1)))
    def _prefetch_next_block():
        wrap = s_blk == n_s - 1
        x_copy(jnp.where(wrap, b_idx + 1, b_idx), jnp.where(wrap, 0, s_blk + 1)).start()

    @pl.when(h == 0)
    def _block_prologue():
        x_copy(b_idx, s_blk).wait()
        x = x_buf[...]
        ms = jnp.mean(x * x, axis=-1, keepdims=True)
        xn = x * lax.rsqrt(ms + EPS) * norms_ref[0:1, :]
        xn_bf = xn.astype(BF16)
        xn_scr[...] = xn_bf
        out_ref[...] = x
        gates = jnp.dot(xn_bf, w_g_ref[...], preferred_element_type=F32)
        gb = gate_b_ref[...]
        for c in range(n_chunks):
            gt = gates[c * CHUNK:(c + 1) * CHUNK, :].T
            li = gt[0:N_HEADS, :] + gb[0:N_HEADS, :]
            lf = _log_sigmoid(gt[N_HEADS:2 * N_HEADS, :] + gb[N_HEADS:2 * N_HEADS, :])
            g_scr[c] = li - _lane_cumsum(lf)
            lf_scr[c] = lf

    proj_scr[0:CARRY, 0:2 * QK] = carry_scr[h]
    conv_w = head_vec_ref[0:CONV_W, :]
    conv_b = head_vec_ref[CONV_W:CONV_W + 1, :]
    mnorm_w = head_vec_ref[CONV_W + 1:CONV_W + 2, :]
    sgu_w = head_vec_ref[CONV_W + 2:CONV_W + 3, :]
    sgu_b = head_vec_ref[CONV_W + 3:CONV_W + 4, :]
    w_sp = jnp.where(causal, spatial_ref[:, 0:CHUNK], 0.0).astype(BF16)
    b_sp = spatial_ref[:, CHUNK:CHUNK + 1]
    k_scale = QK ** -0.5

    state = {"c": c_scr[h],
             "n": n_scr[h][0:1, :],
             "m": m_scr[h][0:1, 0:1]}

    def rows_of(c):
        return pl.ds(CARRY + (c % 2) * CHUNK, CHUNK)

    def tok_of(c):
        return pl.ds(c * CHUNK, CHUNK)

    def project(c, col):
        proj_scr[rows_of(c), col:col + PIECE_COLS] = jnp.dot(
            xn_scr[tok_of(c), :], w_in_ref[:, col:col + PIECE_COLS], preferred_element_type=F32)

    def out_project(c):
        y = y_scr[tok_of(c), :]
        out_ref[tok_of(c), :] += (
            jnp.dot(y[:, 0:DV], w_out_m_ref[...], preferred_element_type=F32)
            + jnp.dot(y[:, DV:2 * DV], w_out_g_ref[...], preferred_element_type=F32))

    def stage_a1(c, st):
        r0 = CARRY + (c % 2) * CHUNK
        if c >= 2 and c % 2 == 0:
            proj_scr[0:CARRY, 0:2 * QK] = proj_scr[2 * CHUNK:2 * CHUNK + CARRY, 0:2 * QK]
        qk_pre = conv_b
        for j in range(CONV_W):
            qk_pre = qk_pre + conv_w[j:j + 1, :] * proj_scr[pl.ds(r0 - (CONV_W - 1) + j, CHUNK),
                                                             C_QK:C_QK + 2 * QK]
        qk_act = _silu(qk_pre)
        q = qk_act[:, 0:QK]
        k = qk_act[:, QK:2 * QK] * k_scale
        st["q_bf"] = q.astype(BF16)
        st["k"], st["k_bf"] = k, k.astype(BF16)

        g_row = g_scr[c, pl.ds(h, 1), :]
        lf_row = lf_scr[c, pl.ds(h, 1), :]
        m11 = state["m"]
        mx = jnp.maximum(jnp.max(jnp.where(causal, g_row, -jnp.inf), axis=1, keepdims=True), m11)
        b_col = jnp.sum(jnp.where(causal, lf_row, 0.0), axis=1, keepdims=True)
        st["p"] = jnp.exp(jnp.where(causal, g_row - mx, -jnp.inf))
        st["decay"] = jnp.exp(m11 - mx)
        st["floor"] = jnp.exp(-(b_col + mx))
        st["qn"] = jnp.sum(q * state["n"], axis=1, keepdims=True)
        state["m"] = b_col[CHUNK - 1:CHUNK, :] + mx[CHUNK - 1:CHUNK, :]

    def stage_a2(c, st):
        st["s_raw"] = lax.dot_general(st["q_bf"], st["k_bf"], (((1,), (1,)), ((), ())),
                                      preferred_element_type=F32)
        st["q_c"] = jnp.dot(st["q_bf"], state["c"].astype(BF16), preferred_element_type=F32)

    def stage_b1(c, st):
        st["v_bf"] = proj_scr[rows_of(c), C_V:C_V + DV].astype(BF16)
        p, decay = st["p"], st["decay"]
        s_qk = st["s_raw"] * p
        st["s_qk_bf"] = s_qk.astype(BF16)
        ws_row = p[CHUNK - 1:CHUNK, :]
        st["a11"] = decay[CHUNK - 1:CHUNK, :]
        st["kt_w"] = (st["k"].T * ws_row).astype(BF16)
        st["ws8"] = jnp.broadcast_to(ws_row, (8, CHUNK)).astype(BF16)
        st["den"] = jnp.sum(s_qk, axis=1, keepdims=True) + decay * st["qn"]

    def stage_b2(c, st):
        v_bf, a11 = st["v_bf"], st["a11"]
        st["num"] = (jnp.dot(st["s_qk_bf"], v_bf, preferred_element_type=F32)
                     + st["decay"] * st["q_c"])
        state["c"] = a11 * state["c"] + jnp.dot(st["kt_w"], v_bf, preferred_element_type=F32)
        state["n"] = a11 * state["n"] + jnp.dot(st["ws8"], st["k_bf"],
                                                preferred_element_type=F32)[0:1, :]

    def stage_c1(c, st):
        hm = st["num"] / jnp.maximum(jnp.abs(st["den"]), st["floor"])
        hm = hm * lax.rsqrt(jnp.mean(hm * hm, axis=-1, keepdims=True) + EPS) * mnorm_w
        y_m = (hm * _sigmoid(proj_scr[rows_of(c), C_O:C_O + DV])
               * _silu(proj_scr[rows_of(c), C_Z:C_Z + DV]))
        y_scr[tok_of(c), 0:DV] = y_m.astype(BF16)

    def stage_d1(c, st):
        gv = _gelu_tanh(proj_scr[rows_of(c), C_VG:C_VG + DV])
        mu = jnp.mean(gv, axis=-1, keepdims=True)
        gc = gv - mu
        var = jnp.mean(gc * gc, axis=-1, keepdims=True)
        st["vgn_bf"] = (gc * lax.rsqrt(var + EPS) * sgu_w + sgu_b).astype(BF16)

    def stage_d2(c, st):
        st["sv"] = jnp.dot(w_sp, st["vgn_bf"], preferred_element_type=F32) + b_sp

    def stage_e1(c, st):
        y_g = (_gelu_tanh(proj_scr[rows_of(c), C_U:C_U + DV]) * st["sv"]
               * _silu(proj_scr[rows_of(c), C_ZG:C_ZG + DV]))
        y_scr[tok_of(c), DV:2 * DV] = y_g.astype(BF16)

    piece_cols = (C_QK, C_V, C_O, C_Z, C_VG, C_U, C_ZG)
    order = ("w", "a2", "w", "b1", "w", "b2", "d1", "w", "c1", "d2", "w", "e1", "a1+", "w", "w")
    order_last = ("a2", "b1", "w", "b2", "d1", "w", "c1", "d2", "w", "e1")
    stage_fn = {"a2": stage_a2, "b1": stage_b1, "b2": stage_b2, "c1": stage_c1, "d1": stage_d1,
                "d2": stage_d2, "e1": stage_e1}
    for col in piece_cols:
        project(0, col)
    chunk_vals = [dict() for _ in range(n_chunks)]
    stage_a1(0, chunk_vals[0])
    for c in range(n_chunks):
        if c + 1 < n_chunks:
            mxu_work = [functools.partial(project, c + 1, col) for col in piece_cols]
            if 1 <= c <= n_chunks - 4:
                mxu_work.insert(4, functools.partial(out_project, c - 1))
        else:
            mxu_work = [functools.partial(out_project, cc) for cc in range(max(c - 3, 0), c)]
        for item in (order if c + 1 < n_chunks else order_last):
            if item == "w":
                if mxu_work:
                    mxu_work.pop(0)()
            elif item == "a1+":
                if c + 1 < n_chunks:
                    stage_a1(c + 1, chunk_vals[c + 1])
            else:
                stage_fn[item](c, chunk_vals[c])
        for work in mxu_work:
            work()
    out_project(n_chunks - 1)

    c_scr[h] = state["c"]
    n_scr[h] = jnp.broadcast_to(state["n"], (8, QK))
    m_scr[h] = jnp.broadcast_to(state["m"], (8, QK))
    last_end = CARRY + ((n_chunks - 1) % 2 + 1) * CHUNK
    carry_scr[h] = proj_scr[last_end - CARRY:last_end, 0:2 * QK]

    @pl.when(h == N_HEADS - 1)
    def _block_epilogue():
        r = out_ref[...]
        ms = jnp.mean(r * r, axis=-1, keepdims=True)
        out_ref[...] = r * lax.rsqrt(ms + EPS) * norms_ref[1:2, :]


def kernel(x, norm_w, w_in, conv_w, conv_b, b_igate, b_fgate, mlstm_norm_w, sgu_norm_w,
           sgu_norm_b, w_spatial, b_spatial, w_out, final_norm_w):
    B, S, D = x.shape
    H = N_HEADS
    d_qk = H * QK
    d_m = H * DV
    assert D == d_m and w_in.shape == (D, 2 * d_qk + 3 * d_m + 2 * H + 3 * d_m)
    assert w_out.shape == (2 * d_m, D) and w_spatial.shape == (H, CHUNK, CHUNK)
    tb = SEQ_BLOCK
    assert S % tb == 0 and tb % CHUNK == 0

    o_q, o_k, o_v, o_o, o_z = 0, d_qk, 2 * d_qk, 2 * d_qk + d_m, 2 * d_qk + 2 * d_m
    o_i = 2 * d_qk + 3 * d_m
    o_u = o_i + 2 * H
    assert o_i % GATE_PAD == 0 and (o_k - o_q) % QK == 0
    assert all(o % DV == 0 for o in (o_v, o_o, o_z))

    assert o_u - o_i == GATE_ROWS and o_i % PREP_ROWS == 0 and PREP_ROWS % GATE_ROWS == 0
    n_head_blocks = H * N_SEGMENTS
    blk = lambda off: off // PREP_ROWS
    per_seg_a1 = (blk(o_q), blk(o_v), blk(o_o), blk(o_z), blk(o_i), blk(o_i + d_m),
                  blk(o_i + 2 * d_m))
    per_seg_a2 = (blk(o_k),) + tuple(a + 1 for a in per_seg_a1[1:])
    per_seg_step = (1, 2, 2, 2, 2, 2, 2)

    def _pick(values, seg):
        out = jnp.int32(values[0])
        for n, v in enumerate(values[1:], start=1):
            out = jnp.where(seg == n, jnp.int32(v), out)
        return out

    def _a_index(per_seg):
        def index_map(t):
            head, seg = t // N_SEGMENTS, t % N_SEGMENTS
            in_head = _pick(per_seg, seg) + _pick(per_seg_step, seg) * head
            gate_blk = blk(o_i) + (0 if per_seg is per_seg_a1 else 1)
            return jnp.where(t == n_head_blocks, gate_blk, in_head), 0
        return index_map

    def _b_index(t):
        a2 = _a_index(per_seg_a2)(jnp.minimum(t, n_head_blocks - 1))[0]
        last = w_in.shape[1] // GATE_ROWS - 1
        return jnp.minimum((a2 + 1) * (PREP_ROWS // GATE_ROWS), last), 0

    w_t = w_in.T
    w_all = pl.pallas_call(
        _weight_prep_kernel,
        out_shape=jax.ShapeDtypeStruct((D, (n_head_blocks + 1) * PIECE_COLS), BF16),
        grid=(n_head_blocks + 1,),
        in_specs=[pl.BlockSpec((PREP_ROWS, D), _a_index(per_seg_a1)),
                  pl.BlockSpec((PREP_ROWS, D), _a_index(per_seg_a2)),
                  pl.BlockSpec((GATE_ROWS, D), _b_index)],
        out_specs=pl.BlockSpec((D, PIECE_COLS), lambda t: (0, t)),
        compiler_params=pltpu.CompilerParams(dimension_semantics=("arbitrary",)),
        name="weight_prep",
    )(w_t, w_t, w_t)

    w_out_bf = w_out.astype(BF16)
    gate_b = jnp.concatenate([b_igate, b_fgate]).astype(F32).reshape(2 * H, 1)
    norms = jnp.stack([norm_w, final_norm_w]).astype(F32)
    conv_qk = jnp.concatenate([conv_w[:, :d_qk].reshape(CONV_W, H, QK),
                               conv_w[:, d_qk:].reshape(CONV_W, H, QK)], axis=-1)
    conv_b_qk = jnp.concatenate([conv_b[:d_qk].reshape(1, H, QK),
                                 conv_b[d_qk:].reshape(1, H, QK)], axis=-1)
    head_vec = jnp.concatenate(
        [conv_qk, conv_b_qk, mlstm_norm_w.reshape(1, H, DV), sgu_norm_w.reshape(1, H, DV),
         sgu_norm_b.reshape(1, H, DV)], axis=0).transpose(1, 0, 2).astype(F32)
    spatial = jnp.concatenate(
        [w_spatial, b_spatial.reshape(H, CHUNK, 1),
         jnp.zeros((H, CHUNK, CHUNK - 1), w_spatial.dtype)], axis=-1).astype(F32)

    grid = (B, S // tb, H)
    head3 = lambda b, s, h: (h, 0, 0)
    fixed2 = lambda b, s, h: (0, 0)
    tok3 = lambda b, s, h: (b, s, 0)

    return pl.pallas_call(
        functools.partial(_layer_kernel, tb=tb),
        out_shape=jax.ShapeDtypeStruct((B, S, D), x.dtype),
        grid=grid,
        in_specs=[
            pl.BlockSpec(memory_space=pl.ANY),
            pl.BlockSpec((2, D), fixed2),
            pl.BlockSpec((D, HEAD_COLS), lambda b, s, h: (0, h)),
            pl.BlockSpec((D, GATE_PAD), lambda b, s, h: (0, H * HEAD_COLS // GATE_PAD)),
            pl.BlockSpec((2 * H, 1), fixed2),
            pl.BlockSpec((None, CONV_W + 4, 2 * QK), head3),
            pl.BlockSpec((None, CHUNK, 2 * CHUNK), head3),
            pl.BlockSpec((DV, D), lambda b, s, h: (h, 0)),
            pl.BlockSpec((DV, D), lambda b, s, h: (H + h, 0)),
        ],
        out_specs=pl.BlockSpec((None, tb, D), tok3),
        scratch_shapes=[
            pltpu.VMEM((tb, D), x.dtype),
            pltpu.SemaphoreType.DMA((1,)),
            pltpu.VMEM((tb, D), BF16),
            pltpu.VMEM((CARRY + 2 * CHUNK, HEAD_COLS), F32),
            pltpu.VMEM((tb // CHUNK, H, CHUNK), F32),
            pltpu.VMEM((tb // CHUNK, H, CHUNK), F32),
            pltpu.VMEM((H, QK, DV), F32),
            pltpu.VMEM((H, 8, QK), F32),
            pltpu.VMEM((H, 8, QK), F32),
            pltpu.VMEM((H, CARRY, 2 * QK), F32),
            pltpu.VMEM((tb, 2 * DV), BF16),
        ],
        compiler_params=pltpu.CompilerParams(
            dimension_semantics=("arbitrary", "arbitrary", "arbitrary"),
            vmem_limit_bytes=VMEM_LIMIT_BYTES),
        name="hybrid_layer",
    )(x, norms, w_all, w_all, gate_b, head_vec, spatial, w_out_bf, w_out_bf)
```

```python
import functools
import math

import jax
import jax.numpy as jnp
from jax import lax
from jax.experimental import pallas as pl
from jax.experimental.pallas import tpu as pltpu

F32 = jnp.float32
BF16 = jnp.bfloat16

N_HEADS = 8
QK = 128
DV = 256
CHUNK = 128
CONV_W = 4
EPS = 1e-6
GATE_PAD = 128
CARRY = 8

C_QK, C_V, C_O, C_Z, C_U, C_VG, C_ZG = 0, 256, 512, 768, 1024, 1280, 1536
HEAD_COLS = 1792
PIECE_COLS = 256

SEQ_BLOCK = 1024
N_SEGMENTS = 7
N_SEGMENTS_FRONT = 4
GATE_ROWS = 16
PREP_ROWS = 128
VMEM_LIMIT_BYTES = 56 * 1024 * 1024


def _sigmoid(x):
    return 1.0 / (1.0 + jnp.exp(-x))


def _silu(x):
    return x * _sigmoid(x)


def _gelu_tanh(x):
    c = math.sqrt(2.0 / math.pi)
    return 0.5 * x * (1.0 + jnp.tanh(c * (x + 0.044715 * (x * x * x))))


def _log_sigmoid(x):
    return jnp.minimum(x, 0.0) - jnp.log1p(jnp.exp(-jnp.abs(x)))


def _lane_cumsum(x):
    lane = lax.broadcasted_iota(jnp.int32, x.shape, 1)
    k = 1
    while k < x.shape[-1]:
        x = x + jnp.where(lane >= k, pltpu.roll(x, k, 1), 0.0)
        k *= 2
    return x


def _weight_prep_kernel(a1_ref, a2_ref, b_ref, out_ref):
    seg = pl.program_id(0) % N_SEGMENTS

    @pl.when(seg < N_SEGMENTS_FRONT)
    def _aligned():
        rows = jnp.concatenate([a1_ref[...], a2_ref[...]], axis=0)
        out_ref[...] = rows.T.astype(BF16)

    @pl.when(seg >= N_SEGMENTS_FRONT)
    def _behind_gates():
        rows = jnp.concatenate([a1_ref[GATE_ROWS:, :], a2_ref[...], b_ref[...]], axis=0)
        out_ref[...] = rows.T.astype(BF16)


def _layer_kernel(x_hbm, norms_ref, w_in_ref, w_g_ref, gate_b_ref, head_vec_ref, spatial_ref,
                  w_out_m_ref, w_out_g_ref,
                  out_ref,
                  x_buf, x_sem, xn_scr, proj_scr, g_scr, lf_scr, c_scr, n_scr, m_scr, carry_scr, y_scr,
                  *, tb):
    b_idx = pl.program_id(0)
    s_blk = pl.program_id(1)
    h = pl.program_id(2)
    n_chunks = tb // CHUNK
    n_b, n_s = pl.num_programs(0), pl.num_programs(1)

    def x_rows(bb, ss):
        return x_hbm.at[pl.ds(pl.multiple_of((bb * n_s + ss) * tb, tb), tb), :]

    def x_copy(bb, ss):
        return pltpu.make_async_copy(x_rows(bb, ss), x_buf, x_sem.at[0])

    def residual_copy():
        return pltpu.make_async_copy(x_rows(b_idx, s_blk), out_ref, x_sem.at[1])

    row_id = lax.broadcasted_iota(jnp.int32, (CHUNK, CHUNK), 0)
    col_id = lax.broadcasted_iota(jnp.int32, (CHUNK, CHUNK), 1)
    causal = col_id <= row_id

    @pl.when(jnp.logical_and(s_blk == 0, h == 0))
    def _reset_state():
        c_scr[...] = jnp.zeros_like(c_scr)
        n_scr[...] = jnp.zeros_like(n_scr)
        m_scr[...] = jnp.zeros_like(m_scr)
        carry_scr[...] = jnp.zeros_like(carry_scr)

    @pl.when(jnp.logical_and(h == 0, jnp.logical_and(b_idx == 0, s_blk == 0)))
    def _fetch_first_block():
        x_copy(0, 0).start()

    @pl.when(jnp.logical_and(h == 1, jnp.logical_or(b_idx < n_b - 1, s_blk < n_s - 1)))
    def _prefetch_next_block():
        wrap = s_blk == n_s - 1
        x_copy(jnp.where(wrap, b_idx + 1, b_idx), jnp.where(wrap, 0, s_blk + 1)).start()

    @pl.when(h == 0)
    def _block_prologue():
        residual_copy().start()
        x_copy(b_idx, s_blk).wait()
        x = x_buf[...]
        ms = jnp.mean(x * x, axis=-1, keepdims=True)
        xn = x * lax.rsqrt(ms + EPS) * norms_ref[0:1, :]
        xn_bf = xn.astype(BF16)
        xn_scr[...] = xn_bf
        gates = jnp.dot(xn_bf, w_g_ref[...], preferred_element_type=F32)
        gb = gate_b_ref[...]
        gts = [gates[c * CHUNK:(c + 1) * CHUNK, :].T for c in range(n_chunks)]
        li_all = jnp.concatenate([gt[0:N_HEADS, :] + gb[0:N_HEADS, :] for gt in gts], axis=0)
        lf_all = _log_sigmoid(jnp.concatenate(
            [gt[N_HEADS:2 * N_HEADS, :] + gb[N_HEADS:2 * N_HEADS, :] for gt in gts], axis=0))
        g_all = li_all - _lane_cumsum(lf_all)
        for c in range(n_chunks):
            g_scr[c] = g_all[c * N_HEADS:(c + 1) * N_HEADS, :]
            lf_scr[c] = lf_all[c * N_HEADS:(c + 1) * N_HEADS, :]
        residual_copy().wait()

    proj_scr[0:CARRY, 0:2 * QK] = carry_scr[h]
    conv_w = head_vec_ref[0:CONV_W, :]
    conv_b = head_vec_ref[CONV_W:CONV_W + 1, :]
    mnorm_w = head_vec_ref[CONV_W + 1:CONV_W + 2, :]
    sgu_w = head_vec_ref[CONV_W + 2:CONV_W + 3, :]
    sgu_b = head_vec_ref[CONV_W + 3:CONV_W + 4, :]
    w_sp = jnp.where(causal, spatial_ref[:, 0:CHUNK], 0.0).astype(BF16)
    b_sp = spatial_ref[:, CHUNK:CHUNK + 1]
    k_scale = QK ** -0.5

    state = {"c": c_scr[h],
             "n": n_scr[h][0:1, :],
             "m": m_scr[h][0:1, 0:1]}

    def rows_of(c):
        return pl.ds(CARRY + (c % 2) * CHUNK, CHUNK)

    def tok_of(c):
        return pl.ds(c * CHUNK, CHUNK)

    def project(c, col):
        proj_scr[rows_of(c), col:col + PIECE_COLS] = jnp.dot(
            xn_scr[tok_of(c), :], w_in_ref[:, col:col + PIECE_COLS], preferred_element_type=F32)

    def out_project(c):
        y = y_scr[tok_of(c), :]
        out_ref[tok_of(c), :] += (
            jnp.dot(y[:, 0:DV], w_out_m_ref[...], preferred_element_type=F32)
            + jnp.dot(y[:, DV:2 * DV], w_out_g_ref[...], preferred_element_type=F32))

    def stage_a1(c, st):
        r0 = CARRY + (c % 2) * CHUNK
        if c >= 2 and c % 2 == 0:
            proj_scr[0:CARRY, 0:2 * QK] = proj_scr[2 * CHUNK:2 * CHUNK + CARRY, 0:2 * QK]
        qk_pre = conv_b
        for j in range(CONV_W):
            qk_pre = qk_pre + conv_w[j:j + 1, :] * proj_scr[pl.ds(r0 - (CONV_W - 1) + j, CHUNK),
                                                             C_QK:C_QK + 2 * QK]
        qk_act = _silu(qk_pre)
        q = qk_act[:, 0:QK]
        k = qk_act[:, QK:2 * QK] * k_scale
        st["q_bf"] = q.astype(BF16)
        st["k"], st["k_bf"] = k, k.astype(BF16)

        g_row = g_scr[c, pl.ds(h, 1), :]
        lf_row = lf_scr[c, pl.ds(h, 1), :]
        m11 = state["m"]
        mx = jnp.maximum(jnp.max(jnp.where(causal, g_row, -jnp.inf), axis=1, keepdims=True), m11)
        b_col = jnp.sum(jnp.where(causal, lf_row, 0.0), axis=1, keepdims=True)
        st["p"] = jnp.exp(jnp.where(causal, g_row - mx, -jnp.inf))
        st["decay"] = jnp.exp(m11 - mx)
        st["floor"] = jnp.exp(-(b_col + mx))
        st["qn"] = jnp.sum(q * state["n"], axis=1, keepdims=True)
        state["m"] = b_col[CHUNK - 1:CHUNK, :] + mx[CHUNK - 1:CHUNK, :]

    def stage_a2(c, st):
        st["s_raw"] = lax.dot_general(st["q_bf"], st["k_bf"], (((1,), (1,)), ((), ())),
                                      preferred_element_type=F32)
        st["q_c"] = jnp.dot(st["q_bf"], state["c"].astype(BF16), preferred_element_type=F32)

    def stage_b1(c, st):
        st["v_bf"] = proj_scr[rows_of(c), C_V:C_V + DV].astype(BF16)
        p, decay = st["p"], st["decay"]
        s_qk = st["s_raw"] * p
        st["s_qk_bf"] = s_qk.astype(BF16)
        ws_row = p[CHUNK - 1:CHUNK, :]
        st["a11"] = decay[CHUNK - 1:CHUNK, :]
        st["kt_w"] = (st["k"].T * ws_row).astype(BF16)
        st["ws8"] = jnp.broadcast_to(ws_row, (8, CHUNK)).astype(BF16)
        st["den"] = jnp.sum(s_qk, axis=1, keepdims=True) + decay * st["qn"]

    def stage_b2(c, st):
        v_bf, a11 = st["v_bf"], st["a11"]
        st["num"] = (jnp.dot(st["s_qk_bf"], v_bf, preferred_element_type=F32)
                     + st["decay"] * st["q_c"])
        state["c"] = a11 * state["c"] + jnp.dot(st["kt_w"], v_bf, preferred_element_type=F32)
        state["n"] = a11 * state["n"] + jnp.dot(st["ws8"], st["k_bf"],
                                                preferred_element_type=F32)[0:1, :]

    def stage_c1(c, st):
        hm = st["num"] / jnp.maximum(jnp.abs(st["den"]), st["floor"])
        hm = hm * lax.rsqrt(jnp.mean(hm * hm, axis=-1, keepdims=True) + EPS) * mnorm_w
        y_m = (hm * _sigmoid(proj_scr[rows_of(c), C_O:C_O + DV])
               * _silu(proj_scr[rows_of(c), C_Z:C_Z + DV]))
        y_scr[tok_of(c), 0:DV] = y_m.astype(BF16)

    def stage_d1(c, st):
        gv = _gelu_tanh(proj_scr[rows_of(c), C_VG:C_VG + DV])
        mu = jnp.mean(gv, axis=-1, keepdims=True)
        gc = gv - mu
        var = jnp.mean(gc * gc, axis=-1, keepdims=True)
        st["vgn_bf"] = (gc * lax.rsqrt(var + EPS) * sgu_w + sgu_b).astype(BF16)

    def stage_d2(c, st):
        st["sv"] = jnp.dot(w_sp, st["vgn_bf"], preferred_element_type=F32) + b_sp

    def stage_e1(c, st):
        y_g = (_gelu_tanh(proj_scr[rows_of(c), C_U:C_U + DV]) * st["sv"]
               * _silu(proj_scr[rows_of(c), C_ZG:C_ZG + DV]))
        y_scr[tok_of(c), DV:2 * DV] = y_g.astype(BF16)

    piece_cols = (C_QK, C_V, C_O, C_Z, C_VG, C_U, C_ZG)
    order = ("w", "a2", "w", "b1", "w", "b2", "d1", "w", "c1", "d2", "w", "e1", "a1+", "w", "w")
    order_last = ("a2", "b1", "w", "b2", "d1", "w", "c1", "d2", "w", "e1")
    stage_fn = {"a2": stage_a2, "b1": stage_b1, "b2": stage_b2, "c1": stage_c1, "d1": stage_d1,
                "d2": stage_d2, "e1": stage_e1}
    for col in piece_cols:
        project(0, col)
    chunk_vals = [dict() for _ in range(n_chunks)]
    stage_a1(0, chunk_vals[0])
    for c in range(n_chunks):
        if c + 1 < n_chunks:
            mxu_work = [functools.partial(project, c + 1, col) for col in piece_cols]
            if 1 <= c <= n_chunks - 4:
                mxu_work.insert(4, functools.partial(out_project, c - 1))
        else:
            mxu_work = [functools.partial(out_project, cc) for cc in range(max(c - 3, 0), c)]
        for item in (order if c + 1 < n_chunks else order_last):
            if item == "w":
                if mxu_work:
                    mxu_work.pop(0)()
            elif item == "a1+":
                if c + 1 < n_chunks:
                    stage_a1(c + 1, chunk_vals[c + 1])
            else:
                stage_fn[item](c, chunk_vals[c])
        for work in mxu_work:
            work()
    out_project(n_chunks - 1)

    c_scr[h] = state["c"]
    n_scr[h] = jnp.broadcast_to(state["n"], (8, QK))
    m_scr[h] = jnp.broadcast_to(state["m"], (8, QK))
    last_end = CARRY + ((n_chunks - 1) % 2 + 1) * CHUNK
    carry_scr[h] = proj_scr[last_end - CARRY:last_end, 0:2 * QK]

    @pl.when(h == N_HEADS - 1)
    def _block_epilogue():
        r = out_ref[...]
        ms = jnp.mean(r * r, axis=-1, keepdims=True)
        out_ref[...] = r * lax.rsqrt(ms + EPS) * norms_ref[1:2, :]


def kernel(x, norm_w, w_in, conv_w, conv_b, b_igate, b_fgate, mlstm_norm_w, sgu_norm_w,
           sgu_norm_b, w_spatial, b_spatial, w_out, final_norm_w):
    B, S, D = x.shape
    H = N_HEADS
    d_qk = H * QK
    d_m = H * DV
    assert D == d_m and w_in.shape == (D, 2 * d_qk + 3 * d_m + 2 * H + 3 * d_m)
    assert w_out.shape == (2 * d_m, D) and w_spatial.shape == (H, CHUNK, CHUNK)
    tb = SEQ_BLOCK
    assert S % tb == 0 and tb % CHUNK == 0

    o_q, o_k, o_v, o_o, o_z = 0, d_qk, 2 * d_qk, 2 * d_qk + d_m, 2 * d_qk + 2 * d_m
    o_i = 2 * d_qk + 3 * d_m
    o_u = o_i + 2 * H
    assert o_i % GATE_PAD == 0 and (o_k - o_q) % QK == 0
    assert all(o % DV == 0 for o in (o_v, o_o, o_z))

    assert o_u - o_i == GATE_ROWS and o_i % PREP_ROWS == 0 and PREP_ROWS % GATE_ROWS == 0
    n_head_blocks = H * N_SEGMENTS
    blk = lambda off: off // PREP_ROWS
    per_seg_a1 = (blk(o_q), blk(o_v), blk(o_o), blk(o_z), blk(o_i), blk(o_i + d_m),
                  blk(o_i + 2 * d_m))
    per_seg_a2 = (blk(o_k),) + tuple(a + 1 for a in per_seg_a1[1:])
    per_seg_step = (1, 2, 2, 2, 2, 2, 2)

    def _pick(values, seg):
        out = jnp.int32(values[0])
        for n, v in enumerate(values[1:], start=1):
            out = jnp.where(seg == n, jnp.int32(v), out)
        return out

    def _a_index(per_seg):
        def index_map(t):
            head, seg = t // N_SEGMENTS, t % N_SEGMENTS
            in_head = _pick(per_seg, seg) + _pick(per_seg_step, seg) * head
            gate_blk = blk(o_i) + (0 if per_seg is per_seg_a1 else 1)
            return jnp.where(t == n_head_blocks, gate_blk, in_head), 0
        return index_map

    def _b_index(t):
        a2 = _a_index(per_seg_a2)(jnp.minimum(t, n_head_blocks - 1))[0]
        last = w_in.shape[1] // GATE_ROWS - 1
        return jnp.minimum((a2 + 1) * (PREP_ROWS // GATE_ROWS), last), 0

    w_t = w_in.T
    w_all = pl.pallas_call(
        _weight_prep_kernel,
        out_shape=jax.ShapeDtypeStruct((D, (n_head_blocks + 1) * PIECE_COLS), BF16),
        grid=(n_head_blocks + 1,),
        in_specs=[pl.BlockSpec((PREP_ROWS, D), _a_index(per_seg_a1)),
                  pl.BlockSpec((PREP_ROWS, D), _a_index(per_seg_a2)),
                  pl.BlockSpec((GATE_ROWS, D), _b_index)],
        out_specs=pl.BlockSpec((D, PIECE_COLS), lambda t: (0, t)),
        compiler_params=pltpu.CompilerParams(dimension_semantics=("arbitrary",)),
        name="weight_prep",
    )(w_t, w_t, w_t)

    w_out_bf = w_out.astype(BF16)
    gate_b = jnp.concatenate([b_igate, b_fgate]).astype(F32).reshape(2 * H, 1)
    norms = jnp.stack([norm_w, final_norm_w]).astype(F32)
    conv_qk = jnp.concatenate([conv_w[:, :d_qk].reshape(CONV_W, H, QK),
                               conv_w[:, d_qk:].reshape(CONV_W, H, QK)], axis=-1)
    conv_b_qk = jnp.concatenate([conv_b[:d_qk].reshape(1, H, QK),
                                 conv_b[d_qk:].reshape(1, H, QK)], axis=-1)
    head_vec = jnp.concatenate(
        [conv_qk, conv_b_qk, mlstm_norm_w.reshape(1, H, DV), sgu_norm_w.reshape(1, H, DV),
         sgu_norm_b.reshape(1, H, DV)], axis=0).transpose(1, 0, 2).astype(F32)
    spatial = jnp.concatenate(
        [w_spatial, b_spatial.reshape(H, CHUNK, 1),
         jnp.zeros((H, CHUNK, CHUNK - 1), w_spatial.dtype)], axis=-1).astype(F32)

    grid = (B, S // tb, H)
    head3 = lambda b, s, h: (h, 0, 0)
    fixed2 = lambda b, s, h: (0, 0)
    tok3 = lambda b, s, h: (b, s, 0)

    out = pl.pallas_call(
        functools.partial(_layer_kernel, tb=tb),
        out_shape=jax.ShapeDtypeStruct((B * S, D), x.dtype),
        grid=grid,
        in_specs=[
            pl.BlockSpec(memory_space=pl.ANY),
            pl.BlockSpec((2, D), fixed2),
            pl.BlockSpec((D, HEAD_COLS), lambda b, s, h: (0, h)),
            pl.BlockSpec((D, GATE_PAD), lambda b, s, h: (0, H * HEAD_COLS // GATE_PAD)),
            pl.BlockSpec((2 * H, 1), fixed2),
            pl.BlockSpec((None, CONV_W + 4, 2 * QK), head3),
            pl.BlockSpec((None, CHUNK, 2 * CHUNK), head3),
            pl.BlockSpec((DV, D), lambda b, s, h: (h, 0)),
            pl.BlockSpec((DV, D), lambda b, s, h: (H + h, 0)),
        ],
        out_specs=pl.BlockSpec((tb, D), lambda b, s, h: (b * (S // tb) + s, 0)),
        scratch_shapes=[
            pltpu.VMEM((tb, D), x.dtype),
            pltpu.SemaphoreType.DMA((2,)),
            pltpu.VMEM((tb, D), BF16),
            pltpu.VMEM((CARRY + 2 * CHUNK, HEAD_COLS), F32),
            pltpu.VMEM((tb // CHUNK, H, CHUNK), F32),
            pltpu.VMEM((tb // CHUNK, H, CHUNK), F32),
            pltpu.VMEM((H, QK, DV), F32),
            pltpu.VMEM((H, 8, QK), F32),
            pltpu.VMEM((H, 8, QK), F32),
            pltpu.VMEM((H, CARRY, 2 * QK), F32),
            pltpu.VMEM((tb, 2 * DV), BF16),
        ],
        compiler_params=pltpu.CompilerParams(
            dimension_semantics=("arbitrary", "arbitrary", "arbitrary"),
            vmem_limit_bytes=VMEM_LIMIT_BYTES),
        name="hybrid_layer",
    )(x.reshape(B * S, D), norms, w_all, w_all, gate_b, head_vec, spatial, w_out_bf, w_out_bf)
    return out.reshape(B, S, D)
```

```python
import functools
import math

import jax
import jax.numpy as jnp
from jax import lax
from jax.experimental import pallas as pl
from jax.experimental.pallas import tpu as pltpu

F32 = jnp.float32
BF16 = jnp.bfloat16

N_HEADS = 8
QK = 128
DV = 256
CHUNK = 128
CONV_W = 4
EPS = 1e-6
GATE_PAD = 128
CARRY = 8

C_QK, C_V, C_O, C_Z, C_U, C_VG, C_ZG = 0, 256, 512, 768, 1024, 1280, 1536
HEAD_COLS = 1792
PIECE_COLS = 256

SEQ_BLOCK = 1024
N_SEGMENTS = 7
N_SEGMENTS_FRONT = 4
GATE_ROWS = 16
PREP_ROWS = 128
VMEM_LIMIT_BYTES = 56 * 1024 * 1024


def _sigmoid(x):
    return 1.0 / (1.0 + jnp.exp(-x))


def _silu(x):
    return x * _sigmoid(x)


def _gelu_tanh(x):
    c = math.sqrt(2.0 / math.pi)
    return 0.5 * x * (1.0 + jnp.tanh(c * (x + 0.044715 * (x * x * x))))


def _log_sigmoid(x):
    return jnp.minimum(x, 0.0) - jnp.log1p(jnp.exp(-jnp.abs(x)))


def _lane_cumsum(x):
    lane = lax.broadcasted_iota(jnp.int32, x.shape, 1)
    k = 1
    while k < x.shape[-1]:
        x = x + jnp.where(lane >= k, pltpu.roll(x, k, 1), 0.0)
        k *= 2
    return x


def _weight_prep_kernel(a1_ref, a2_ref, b_ref, out_ref):
    seg = pl.program_id(0) % N_SEGMENTS

    @pl.when(seg < N_SEGMENTS_FRONT)
    def _aligned():
        rows = jnp.concatenate([a1_ref[...], a2_ref[...]], axis=0)
        out_ref[...] = rows.T.astype(BF16)

    @pl.when(seg >= N_SEGMENTS_FRONT)
    def _behind_gates():
        rows = jnp.concatenate([a1_ref[GATE_ROWS:, :], a2_ref[...], b_ref[...]], axis=0)
        out_ref[...] = rows.T.astype(BF16)


def _layer_kernel(x_hbm, norms_ref, w_in_ref, w_g_ref, gate_b_ref, head_vec_ref, spatial_ref,
                  w_out_m_ref, w_out_g_ref,
                  out_ref,
                  x_buf, x_sem, xn_scr, proj_scr, g_scr, lf_scr, c_scr, n_scr, m_scr, carry_scr, y_scr,
                  *, tb):
    b_idx = pl.program_id(0)
    s_blk = pl.program_id(1)
    h = pl.program_id(2)
    n_chunks = tb // CHUNK
    n_b, n_s = pl.num_programs(0), pl.num_programs(1)

    def x_copy(bb, ss):
        return pltpu.make_async_copy(x_hbm.at[bb, pl.ds(pl.multiple_of(ss * tb, tb), tb), :],
                                     x_buf, x_sem.at[0])

    row_id = lax.broadcasted_iota(jnp.int32, (CHUNK, CHUNK), 0)
    col_id = lax.broadcasted_iota(jnp.int32, (CHUNK, CHUNK), 1)
    causal = col_id <= row_id

    @pl.when(jnp.logical_and(s_blk == 0, h == 0))
    def _reset_state():
        c_scr[...] = jnp.zeros_like(c_scr)
        n_scr[...] = jnp.zeros_like(n_scr)
        m_scr[...] = jnp.zeros_like(m_scr)
        carry_scr[...] = jnp.zeros_like(carry_scr)

    @pl.when(jnp.logical_and(h == 0, jnp.logical_and(b_idx == 0, s_blk == 0)))
    def _fetch_first_block():
        x_copy(0, 0).start()

    @pl.when(jnp.logical_and(h == 1, jnp.logical_or(b_idx < n_b - 1, s_blk < n_s - 1)))
    def _prefetch_next_block():
        wrap = s_blk == n_s - 1
        x_copy(jnp.where(wrap, b_idx + 1, b_idx), jnp.where(wrap, 0, s_blk + 1)).start()

    @pl.when(h == 0)
    def _block_prologue():
        x_copy(b_idx, s_blk).wait()
        res_copy = pltpu.make_async_copy(x_buf, out_ref, x_sem.at[1])
        res_copy.start()
        x = x_buf[...]
        ms = jnp.mean(x * x, axis=-1, keepdims=True)
        xn = x * lax.rsqrt(ms + EPS) * norms_ref[0:1, :]
        xn_bf = xn.astype(BF16)
        xn_scr[...] = xn_bf
        gates = jnp.dot(xn_bf, w_g_ref[...], preferred_element_type=F32)
        gb = gate_b_ref[...]
        for c in range(n_chunks):
            gt = gates[c * CHUNK:(c + 1) * CHUNK, :].T
            li = gt[0:N_HEADS, :] + gb[0:N_HEADS, :]
            lf = _log_sigmoid(gt[N_HEADS:2 * N_HEADS, :] + gb[N_HEADS:2 * N_HEADS, :])
            g_scr[c] = li - _lane_cumsum(lf)
            lf_scr[c] = lf
        res_copy.wait()

    proj_scr[0:CARRY, 0:2 * QK] = carry_scr[h]
    conv_w = head_vec_ref[0:CONV_W, :]
    conv_b = head_vec_ref[CONV_W:CONV_W + 1, :]
    mnorm_w = head_vec_ref[CONV_W + 1:CONV_W + 2, :]
    sgu_w = head_vec_ref[CONV_W + 2:CONV_W + 3, :]
    sgu_b = head_vec_ref[CONV_W + 3:CONV_W + 4, :]
    w_sp = jnp.where(causal, spatial_ref[:, 0:CHUNK], 0.0).astype(BF16)
    b_sp = spatial_ref[:, CHUNK:CHUNK + 1]
    k_scale = QK ** -0.5

    state = {"c": c_scr[h],
             "n": n_scr[h][0:1, :],
             "m": m_scr[h][0:1, 0:1]}

    def rows_of(c):
        return pl.ds(CARRY + (c % 2) * CHUNK, CHUNK)

    def tok_of(c):
        return pl.ds(c * CHUNK, CHUNK)

    def project(c, col):
        proj_scr[rows_of(c), col:col + PIECE_COLS] = jnp.dot(
            xn_scr[tok_of(c), :], w_in_ref[:, col:col + PIECE_COLS], preferred_element_type=F32)

    def out_project(c):
        y = y_scr[tok_of(c), :]
        out_ref[tok_of(c), :] += (
            jnp.dot(y[:, 0:DV], w_out_m_ref[...], preferred_element_type=F32)
            + jnp.dot(y[:, DV:2 * DV], w_out_g_ref[...], preferred_element_type=F32))

    def stage_a1(c, st):
        r0 = CARRY + (c % 2) * CHUNK
        if c >= 2 and c % 2 == 0:
            proj_scr[0:CARRY, 0:2 * QK] = proj_scr[2 * CHUNK:2 * CHUNK + CARRY, 0:2 * QK]
        qk_pre = conv_b
        for j in range(CONV_W):
            qk_pre = qk_pre + conv_w[j:j + 1, :] * proj_scr[pl.ds(r0 - (CONV_W - 1) + j, CHUNK),
                                                             C_QK:C_QK + 2 * QK]
        qk_act = _silu(qk_pre)
        q = qk_act[:, 0:QK]
        k = qk_act[:, QK:2 * QK] * k_scale
        st["q_bf"] = q.astype(BF16)
        st["k"], st["k_bf"] = k, k.astype(BF16)

        g_row = g_scr[c, pl.ds(h, 1), :]
        lf_row = lf_scr[c, pl.ds(h, 1), :]
        m11 = state["m"]
        mx = jnp.maximum(jnp.max(jnp.where(causal, g_row, -jnp.inf), axis=1, keepdims=True), m11)
        b_col = jnp.sum(jnp.where(causal, lf_row, 0.0), axis=1, keepdims=True)
        st["p"] = jnp.exp(jnp.where(causal, g_row - mx, -jnp.inf))
        st["decay"] = jnp.exp(m11 - mx)
        st["floor"] = jnp.exp(-(b_col + mx))
        st["qn"] = jnp.sum(q * state["n"], axis=1, keepdims=True)
        state["m"] = b_col[CHUNK - 1:CHUNK, :] + mx[CHUNK - 1:CHUNK, :]

    def stage_a2(c, st):
        st["s_raw"] = lax.dot_general(st["q_bf"], st["k_bf"], (((1,), (1,)), ((), ())),
                                      preferred_element_type=F32)
        st["q_c"] = jnp.dot(st["q_bf"], state["c"].astype(BF16), preferred_element_type=F32)

    def stage_b1(c, st):
        st["v_bf"] = proj_scr[rows_of(c), C_V:C_V + DV].astype(BF16)
        p, decay = st["p"], st["decay"]
        s_qk = st["s_raw"] * p
        st["s_qk_bf"] = s_qk.astype(BF16)
        ws_row = p[CHUNK - 1:CHUNK, :]
        st["a11"] = decay[CHUNK - 1:CHUNK, :]
        st["kt_w"] = (st["k"].T * ws_row).astype(BF16)
        st["ws8"] = jnp.broadcast_to(ws_row, (8, CHUNK)).astype(BF16)
        st["den"] = jnp.sum(s_qk, axis=1, keepdims=True) + decay * st["qn"]

    def stage_b2(c, st):
        v_bf, a11 = st["v_bf"], st["a11"]
        st["num"] = (jnp.dot(st["s_qk_bf"], v_bf, preferred_element_type=F32)
                     + st["decay"] * st["q_c"])
        state["c"] = a11 * state["c"] + jnp.dot(st["kt_w"], v_bf, preferred_element_type=F32)
        state["n"] = a11 * state["n"] + jnp.dot(st["ws8"], st["k_bf"],
                                                preferred_element_type=F32)[0:1, :]

    def stage_c1(c, st):
        hm = st["num"] / jnp.maximum(jnp.abs(st["den"]), st["floor"])
        hm = hm * lax.rsqrt(jnp.mean(hm * hm, axis=-1, keepdims=True) + EPS) * mnorm_w
        y_m = (hm * _sigmoid(proj_scr[rows_of(c), C_O:C_O + DV])
               * _silu(proj_scr[rows_of(c), C_Z:C_Z + DV]))
        y_scr[tok_of(c), 0:DV] = y_m.astype(BF16)

    def stage_d1(c, st):
        gv = _gelu_tanh(proj_scr[rows_of(c), C_VG:C_VG + DV])
        mu = jnp.mean(gv, axis=-1, keepdims=True)
        gc = gv - mu
        var = jnp.mean(gc * gc, axis=-1, keepdims=True)
        st["vgn_bf"] = (gc * lax.rsqrt(var + EPS) * sgu_w + sgu_b).astype(BF16)

    def stage_d2(c, st):
        st["sv"] = jnp.dot(w_sp, st["vgn_bf"], preferred_element_type=F32) + b_sp

    def stage_e1(c, st):
        y_g = (_gelu_tanh(proj_scr[rows_of(c), C_U:C_U + DV]) * st["sv"]
               * _silu(proj_scr[rows_of(c), C_ZG:C_ZG + DV]))
        y_scr[tok_of(c), DV:2 * DV] = y_g.astype(BF16)

    piece_cols = (C_QK, C_V, C_O, C_Z, C_VG, C_U, C_ZG)
    order = ("w", "a2", "w", "b1", "w", "b2", "d1", "w", "c1", "d2", "w", "e1", "a1+", "w", "w")
    order_last = ("a2", "b1", "w", "b2", "d1", "w", "c1", "d2", "w", "e1")
    stage_fn = {"a2": stage_a2, "b1": stage_b1, "b2": stage_b2, "c1": stage_c1, "d1": stage_d1,
                "d2": stage_d2, "e1": stage_e1}
    for col in piece_cols:
        project(0, col)
    chunk_vals = [dict() for _ in range(n_chunks)]
    stage_a1(0, chunk_vals[0])
    for c in range(n_chunks):
        if c + 1 < n_chunks:
            mxu_work = [functools.partial(project, c + 1, col) for col in piece_cols]
            if 1 <= c <= n_chunks - 4:
                mxu_work.insert(4, functools.partial(out_project, c - 1))
        else:
            mxu_work = [functools.partial(out_project, cc) for cc in range(max(c - 3, 0), c)]
        for item in (order if c + 1 < n_chunks else order_last):
            if item == "w":
                if mxu_work:
                    mxu_work.pop(0)()
            elif item == "a1+":
                if c + 1 < n_chunks:
                    stage_a1(c + 1, chunk_vals[c + 1])
            else:
                stage_fn[item](c, chunk_vals[c])
        for work in mxu_work:
            work()
    out_project(n_chunks - 1)

    c_scr[h] = state["c"]
    n_scr[h] = jnp.broadcast_to(state["n"], (8, QK))
    m_scr[h] = jnp.broadcast_to(state["m"], (8, QK))
    last_end = CARRY + ((n_chunks - 1) % 2 + 1) * CHUNK
    carry_scr[h] = proj_scr[last_end - CARRY:last_end, 0:2 * QK]

    @pl.when(h == N_HEADS - 1)
    def _block_epilogue():
        r = out_ref[...]
        ms = jnp.mean(r * r, axis=-1, keepdims=True)
        out_ref[...] = r * lax.rsqrt(ms + EPS) * norms_ref[1:2, :]


def kernel(x, norm_w, w_in, conv_w, conv_b, b_igate, b_fgate, mlstm_norm_w, sgu_norm_w,
           sgu_norm_b, w_spatial, b_spatial, w_out, final_norm_w):
    B, S, D = x.shape
    H = N_HEADS
    d_qk = H * QK
    d_m = H * DV
    assert D == d_m and w_in.shape == (D, 2 * d_qk + 3 * d_m + 2 * H + 3 * d_m)
    assert w_out.shape == (2 * d_m, D) and w_spatial.shape == (H, CHUNK, CHUNK)
    tb = SEQ_BLOCK
    assert S % tb == 0 and tb % CHUNK == 0

    o_q, o_k, o_v, o_o, o_z = 0, d_qk, 2 * d_qk, 2 * d_qk + d_m, 2 * d_qk + 2 * d_m
    o_i = 2 * d_qk + 3 * d_m
    o_u = o_i + 2 * H
    assert o_i % GATE_PAD == 0 and (o_k - o_q) % QK == 0
    assert all(o % DV == 0 for o in (o_v, o_o, o_z))

    assert o_u - o_i == GATE_ROWS and o_i % PREP_ROWS == 0 and PREP_ROWS % GATE_ROWS == 0
    n_head_blocks = H * N_SEGMENTS
    blk = lambda off: off // PREP_ROWS
    per_seg_a1 = (blk(o_q), blk(o_v), blk(o_o), blk(o_z), blk(o_i), blk(o_i + d_m),
                  blk(o_i + 2 * d_m))
    per_seg_a2 = (blk(o_k),) + tuple(a + 1 for a in per_seg_a1[1:])
    per_seg_step = (1, 2, 2, 2, 2, 2, 2)

    def _pick(values, seg):
        out = jnp.int32(values[0])
        for n, v in enumerate(values[1:], start=1):
            out = jnp.where(seg == n, jnp.int32(v), out)
        return out

    def _a_index(per_seg):
        def index_map(t):
            head, seg = t // N_SEGMENTS, t % N_SEGMENTS
            in_head = _pick(per_seg, seg) + _pick(per_seg_step, seg) * head
            gate_blk = blk(o_i) + (0 if per_seg is per_seg_a1 else 1)
            return jnp.where(t == n_head_blocks, gate_blk, in_head), 0
        return index_map

    def _b_index(t):
        a2 = _a_index(per_seg_a2)(jnp.minimum(t, n_head_blocks - 1))[0]
        last = w_in.shape[1] // GATE_ROWS - 1
        return jnp.minimum((a2 + 1) * (PREP_ROWS // GATE_ROWS), last), 0

    w_t = w_in.T
    w_all = pl.pallas_call(
        _weight_prep_kernel,
        out_shape=jax.ShapeDtypeStruct((D, (n_head_blocks + 1) * PIECE_COLS), BF16),
        grid=(n_head_blocks + 1,),
        in_specs=[pl.BlockSpec((PREP_ROWS, D), _a_index(per_seg_a1)),
                  pl.BlockSpec((PREP_ROWS, D), _a_index(per_seg_a2)),
                  pl.BlockSpec((GATE_ROWS, D), _b_index)],
        out_specs=pl.BlockSpec((D, PIECE_COLS), lambda t: (0, t)),
        compiler_params=pltpu.CompilerParams(dimension_semantics=("arbitrary",)),
        name="weight_prep",
    )(w_t, w_t, w_t)

    w_out_bf = w_out.astype(BF16)
    gate_b = jnp.concatenate([b_igate, b_fgate]).astype(F32).reshape(2 * H, 1)
    norms = jnp.stack([norm_w, final_norm_w]).astype(F32)
    conv_qk = jnp.concatenate([conv_w[:, :d_qk].reshape(CONV_W, H, QK),
                               conv_w[:, d_qk:].reshape(CONV_W, H, QK)], axis=-1)
    conv_b_qk = jnp.concatenate([conv_b[:d_qk].reshape(1, H, QK),
                                 conv_b[d_qk:].reshape(1, H, QK)], axis=-1)
    head_vec = jnp.concatenate(
        [conv_qk, conv_b_qk, mlstm_norm_w.reshape(1, H, DV), sgu_norm_w.reshape(1, H, DV),
         sgu_norm_b.reshape(1, H, DV)], axis=0).transpose(1, 0, 2).astype(F32)
    spatial = jnp.concatenate(
        [w_spatial, b_spatial.reshape(H, CHUNK, 1),
         jnp.zeros((H, CHUNK, CHUNK - 1), w_spatial.dtype)], axis=-1).astype(F32)

    grid = (B, S // tb, H)
    head3 = lambda b, s, h: (h, 0, 0)
    fixed2 = lambda b, s, h: (0, 0)
    tok3 = lambda b, s, h: (b, s, 0)

    out = pl.pallas_call(
        functools.partial(_layer_kernel, tb=tb),
        out_shape=jax.ShapeDtypeStruct((B * S, D), x.dtype),
        grid=grid,
        in_specs=[
            pl.BlockSpec(memory_space=pl.ANY),
            pl.BlockSpec((2, D), fixed2),
            pl.BlockSpec((D, HEAD_COLS), lambda b, s, h: (0, h)),
            pl.BlockSpec((D, GATE_PAD), lambda b, s, h: (0, H * HEAD_COLS // GATE_PAD)),
            pl.BlockSpec((2 * H, 1), fixed2),
            pl.BlockSpec((None, CONV_W + 4, 2 * QK), head3),
            pl.BlockSpec((None, CHUNK, 2 * CHUNK), head3),
            pl.BlockSpec((DV, D), lambda b, s, h: (h, 0)),
            pl.BlockSpec((DV, D), lambda b, s, h: (H + h, 0)),
        ],
        out_specs=pl.BlockSpec((tb, D), lambda b, s, h: (b * (S // tb) + s, 0)),
        scratch_shapes=[
            pltpu.VMEM((tb, D), x.dtype),
            pltpu.SemaphoreType.DMA((2,)),
            pltpu.VMEM((tb, D), BF16),
            pltpu.VMEM((CARRY + 2 * CHUNK, HEAD_COLS), F32),
            pltpu.VMEM((tb // CHUNK, H, CHUNK), F32),
            pltpu.VMEM((tb // CHUNK, H, CHUNK), F32),
            pltpu.VMEM((H, QK, DV), F32),
            pltpu.VMEM((H, 8, QK), F32),
            pltpu.VMEM((H, 8, QK), F32),
            pltpu.VMEM((H, CARRY, 2 * QK), F32),
            pltpu.VMEM((tb, 2 * DV), BF16),
        ],
        compiler_params=pltpu.CompilerParams(
            dimension_semantics=("arbitrary", "arbitrary", "arbitrary"),
            vmem_limit_bytes=VMEM_LIMIT_BYTES),
        name="hybrid_layer",
    )(x, norms, w_all, w_all, gate_b, head_vec, spatial, w_out_bf, w_out_bf)
    return out.reshape(B, S, D)
```
